```python
import jax, jax.numpy as jnp
from jax import lax
import numpy as np

D_MODEL = 2048
BATCH = 2
SEQ = 4096
DEPTH = 4

GRID_W = 64
CTX_LEN = 256
N_MIXERS = 3
N_ATTN_LAYERS = (DEPTH + 2) // 3
N_FNET_LAYERS = (DEPTH + 1) // 3
N_MLA_LAYERS = DEPTH // 3

BLK = 128
WINDOW = 128
A_HQ = 32
A_HKV = 4
A_GROUPS = A_HQ // A_HKV
A_DH = 64
A_QKV_DIM = (A_HQ + 2 * A_HKV) * A_DH
A_SCALE = A_DH ** -0.5

F_GROUPS = 8
F_GROUP_DIM = D_MODEL // F_GROUPS

M_H = 16
M_QR = 512
M_KVR = 512
M_NOPE = 128
M_ROPE = 64
M_V = 128
M_SCALE = (M_NOPE + M_ROPE) ** -0.5

N_EXPERTS = 16
N_EXPERT_GROUPS = 4
EXPERTS_PER_GROUP = N_EXPERTS // N_EXPERT_GROUPS
TOP_K = 2
D_FF_EXPERT = 1408

ROPE_THETA = 10000.0
NORM_EPS = 1e-6
NEG_INF = -1e30

kernel_name = "hybrid_dit_swa_fnet_mla_moe"


def rmsnorm(x, g):
    x32 = x.astype(jnp.float32)
    y = x32 * lax.rsqrt(jnp.mean(x32 * x32, axis=-1, keepdims=True) + NORM_EPS)
    return (y * g.astype(jnp.float32)).astype(x.dtype)


def modulate(h, shift, scale):
    return h * (1.0 + scale) + shift


def axial_angles(pos_r, pos_c, dim):
    d_axis = dim // 2
    inv_freq = ROPE_THETA ** (-jnp.arange(0, d_axis, 2, dtype=jnp.float32) / d_axis)
    return (pos_r.astype(jnp.float32)[:, None] * inv_freq,
            pos_c.astype(jnp.float32)[:, None] * inv_freq)


def _rotate(x, ang):
    cos = jnp.cos(ang)[None, :, None, :].astype(x.dtype)
    sin = jnp.sin(ang)[None, :, None, :].astype(x.dtype)
    x1, x2 = jnp.split(x, 2, axis=-1)
    return jnp.concatenate([x1 * cos - x2 * sin, x2 * cos + x1 * sin], axis=-1)


def rope_2d(x, ang_r, ang_c):
    d_axis = x.shape[-1] // 2
    return jnp.concatenate([_rotate(x[..., :d_axis], ang_r), _rotate(x[..., d_axis:], ang_c)], axis=-1)


def windowed_gqa(h_l, h_c, w_qkv, sink, w_o, pos_r, pos_c, need_ctx):
    B, S, _ = h_l.shape
    C = h_c.shape[1]
    nb = S // BLK

    def proj(h):
        L = h.shape[1]
        q, k, v = jnp.split(h @ w_qkv, [A_HQ * A_DH, (A_HQ + A_HKV) * A_DH], axis=-1)
        return (q.reshape(B, L, A_HQ, A_DH), k.reshape(B, L, A_HKV, A_DH), v.reshape(B, L, A_HKV, A_DH))

    q, k, v = proj(h_l)
    qc, kc, vc = proj(h_c)
    ang_r, ang_c = axial_angles(pos_r, pos_c, A_DH)
    q = rope_2d(q, ang_r, ang_c).reshape(B, S, A_HKV, A_GROUPS, A_DH)
    k = rope_2d(k, ang_r, ang_c)
    qc = qc.reshape(B, C, A_HKV, A_GROUPS, A_DH)
    pad = ((0, 0), (BLK, BLK), (0, 0), (0, 0))
    k_pad = jnp.pad(k, pad)
    v_pad = jnp.pad(v, pad)
    sink_f = sink.astype(jnp.float32).reshape(A_HKV, A_GROUPS)[None, :, :, None, None]
    offs_q = jnp.arange(BLK)
    offs_k = jnp.arange(3 * BLK)

    def block(n):
        start = n * BLK
        q_n = lax.dynamic_slice_in_dim(q, start, BLK, axis=1)
        k_n = lax.dynamic_slice_in_dim(k_pad, start, 3 * BLK, axis=1)
        v_n = lax.dynamic_slice_in_dim(v_pad, start, 3 * BLK, axis=1)
        qpos = start + offs_q
        kpos = start - BLK + offs_k
        mask = (jnp.abs(qpos[:, None] - kpos[None, :]) <= WINDOW) & ((kpos >= 0) & (kpos < S))[None, :]
        s_loc = jnp.einsum('bqhgd,bkhd->bhgqk', q_n, k_n).astype(jnp.float32) * A_SCALE
        s_loc = jnp.where(mask, s_loc, NEG_INF)
        s_ctx = jnp.einsum('bqhgd,bkhd->bhgqk', q_n, kc).astype(jnp.float32) * A_SCALE
        s_sink = jnp.broadcast_to(sink_f, s_loc.shape[:-1] + (1,))
        p = jax.nn.softmax(jnp.concatenate([s_loc, s_ctx, s_sink], axis=-1), axis=-1).astype(v.dtype)
        return (jnp.einsum('bhgqk,bkhd->bqhgd', p[..., :3 * BLK], v_n)
                + jnp.einsum('bhgqk,bkhd->bqhgd', p[..., 3 * BLK:3 * BLK + C], vc))

    o = lax.map(block, jnp.arange(nb))
    out_l = jnp.moveaxis(o, 0, 1).reshape(B, S, A_HQ * A_DH) @ w_o
    out_c = None
    if need_ctx:
        s = jnp.einsum('bqhgd,bkhd->bhgqk', qc, kc).astype(jnp.float32) * A_SCALE
        s_sink = jnp.broadcast_to(sink_f, s.shape[:-1] + (1,))
        p = jax.nn.softmax(jnp.concatenate([s, s_sink], axis=-1), axis=-1)[..., :C].astype(vc.dtype)
        out_c = jnp.einsum('bhgqk,bkhd->bqhgd', p, vc).reshape(B, C, A_HQ * A_DH) @ w_o
    return out_l, out_c


def fourier_mix(h_l, h_c, w_o, need_ctx):
    def mix(h):
        B, L, D = h.shape
        hg = h.astype(jnp.float32).reshape(B, L, F_GROUPS, F_GROUP_DIM)
        y = jnp.fft.fft2(hg, axes=(1, 3), norm="ortho").real
        return y.reshape(B, L, D).astype(h.dtype) @ w_o

    out_l = mix(h_l)
    out_c = mix(h_c) if need_ctx else None
    return out_l, out_c


def mla(h_l, h_c, w_in, g_q, g_kv, w_uq, w_ukv, w_o, pos_r, pos_c, need_ctx):
    B, S, _ = h_l.shape
    C = h_c.shape[1]
    nb = S // BLK
    ang_r, ang_c = axial_angles(pos_r, pos_c, M_ROPE)

    def proj(h, with_rope):
        L = h.shape[1]
        cq, ckv, kr = jnp.split(h @ w_in, [M_QR, M_QR + M_KVR], axis=-1)
        q = (rmsnorm(cq, g_q) @ w_uq).reshape(B, L, M_H, M_NOPE + M_ROPE)
        kv = (rmsnorm(ckv, g_kv) @ w_ukv).reshape(B, L, M_H, M_NOPE + M_V)
        q_nope, q_rope = jnp.split(q, [M_NOPE], axis=-1)
        k_nope, v = jnp.split(kv, [M_NOPE], axis=-1)
        kr = kr[:, :, None, :]
        if with_rope:
            q_rope = rope_2d(q_rope, ang_r, ang_c)
            kr = rope_2d(kr, ang_r, ang_c)
        q = jnp.concatenate([q_nope, q_rope], axis=-1)
        k = jnp.concatenate([k_nope, jnp.broadcast_to(kr, (B, L, M_H, M_ROPE))], axis=-1)
        return q, k, v

    q, k, v = proj(h_l, True)
    qc, kc, vc = proj(h_c, False)
    k_all = jnp.concatenate([k, kc], axis=1)
    v_all = jnp.concatenate([v, vc], axis=1)

    def block(n):
        q_n = lax.dynamic_slice_in_dim(q, n * BLK, BLK, axis=1)
        s = jnp.einsum('bqhd,bkhd->bhqk', q_n, k_all).astype(jnp.float32) * M_SCALE
        p = jax.nn.softmax(s, axis=-1).astype(v_all.dtype)
        return jnp.einsum('bhqk,bkhd->bqhd', p, v_all)

    o = lax.map(block, jnp.arange(nb))
    out_l = jnp.moveaxis(o, 0, 1).reshape(B, S, M_H * M_V) @ w_o
    out_c = None
    if need_ctx:
        s = jnp.einsum('bqhd,bkhd->bhqk', qc, kc).astype(jnp.float32) * M_SCALE
        p = jax.nn.softmax(s, axis=-1).astype(vc.dtype)
        out_c = jnp.einsum('bhqk,bkhd->bqhd', p, vc).reshape(B, C, M_H * M_V) @ w_o
    return out_l, out_c


def moe(h, router_w, router_b, w_gate, w_up, w_down):
    N = h.shape[0]
    scores = jax.nn.sigmoid((h @ router_w).astype(jnp.float32))
    sel = scores + router_b.astype(jnp.float32)
    grp = sel.reshape(N, N_EXPERT_GROUPS, EXPERTS_PER_GROUP)
    grp_score = lax.top_k(grp, TOP_K)[0].sum(axis=-1)
    best = jnp.argmax(grp_score, axis=-1)
    in_group = (jnp.arange(N_EXPERTS) // EXPERTS_PER_GROUP)[None, :] == best[:, None]
    _, idx = lax.top_k(jnp.where(in_group, sel, NEG_INF), TOP_K)
    w = jnp.take_along_axis(scores, idx, axis=-1)
    w = w / jnp.sum(w, axis=-1, keepdims=True)
    gates = jnp.sum(jax.nn.one_hot(idx, N_EXPERTS, dtype=jnp.float32) * w[..., None], axis=1).astype(h.dtype)
    y = jnp.zeros((N, w_down.shape[-1]), h.dtype)
    for e in range(N_EXPERTS):
        a = jax.nn.silu(h @ w_gate[e]) * (h @ w_up[e])
        y = y + gates[:, e:e + 1] * (a @ w_down[e])
    return y


def setup_inputs(seed: int = 0) -> dict:
    key = jax.random.key(seed)
    ks = jax.random.split(key, 24)
    f32 = jnp.float32

    def nrm(k, shape, scale):
        return jax.random.normal(k, shape, f32) * scale

    return {
        "x": nrm(ks[0], (BATCH, SEQ, D_MODEL), 1.0),
        "c": nrm(ks[1], (BATCH, D_MODEL), 1.0),
        "ctx": nrm(ks[2], (BATCH, CTX_LEN, D_MODEL), 1.0),
        "c_ctx": nrm(ks[3], (D_MODEL,), 1.0),
        "ada_w": nrm(ks[4], (DEPTH, D_MODEL, 6 * D_MODEL), 0.5 * D_MODEL ** -0.5),
        "ada_b": nrm(ks[5], (DEPTH, 6 * D_MODEL), 0.02),
        "norm_g": 1.0 + nrm(ks[6], (DEPTH, 2, D_MODEL), 0.02),
        "final_g": 1.0 + nrm(ks[7], (D_MODEL,), 0.02),
        "attn_w_qkv": nrm(ks[8], (N_ATTN_LAYERS, D_MODEL, A_QKV_DIM), D_MODEL ** -0.5),
        "attn_sink": nrm(ks[9], (N_ATTN_LAYERS, A_HQ), 0.5),
        "attn_w_o": nrm(ks[10], (N_ATTN_LAYERS, A_HQ * A_DH, D_MODEL), (A_HQ * A_DH) ** -0.5),
        "fnet_w_o": nrm(ks[11], (N_FNET_LAYERS, D_MODEL, D_MODEL), D_MODEL ** -0.5),
        "mla_w_in": nrm(ks[12], (N_MLA_LAYERS, D_MODEL, M_QR + M_KVR + M_ROPE), D_MODEL ** -0.5),
        "mla_g_q": 1.0 + nrm(ks[13], (N_MLA_LAYERS, M_QR), 0.02),
        "mla_g_kv": 1.0 + nrm(ks[14], (N_MLA_LAYERS, M_KVR), 0.02),
        "mla_w_uq": nrm(ks[15], (N_MLA_LAYERS, M_QR, M_H * (M_NOPE + M_ROPE)), M_QR ** -0.5),
        "mla_w_ukv": nrm(ks[16], (N_MLA_LAYERS, M_KVR, M_H * (M_NOPE + M_V)), M_KVR ** -0.5),
        "mla_w_o": nrm(ks[17], (N_MLA_LAYERS, M_H * M_V, D_MODEL), (M_H * M_V) ** -0.5),
        "router_w": nrm(ks[18], (D_MODEL, N_EXPERTS), D_MODEL ** -0.5),
        "router_b": nrm(ks[19], (N_EXPERTS,), 0.01),
        "moe_w_gate": nrm(ks[20], (DEPTH, N_EXPERTS, D_MODEL, D_FF_EXPERT), D_MODEL ** -0.5),
        "moe_w_up": nrm(ks[21], (DEPTH, N_EXPERTS, D_MODEL, D_FF_EXPERT), D_MODEL ** -0.5),
        "moe_w_down": nrm(ks[22], (DEPTH, N_EXPERTS, D_FF_EXPERT, D_MODEL), D_FF_EXPERT ** -0.5),
    }


def reference(x, c, ctx, c_ctx, ada_w, ada_b, norm_g, final_g, attn_w_qkv, attn_sink, attn_w_o,
              fnet_w_o, mla_w_in, mla_g_q, mla_g_kv, mla_w_uq, mla_w_ukv, mla_w_o,
              router_w, router_b, moe_w_gate, moe_w_up, moe_w_down):
    B, S, D = x.shape
    C = ctx.shape[1]
    ROWS = S // GRID_W
    pos_r = jnp.repeat(jnp.arange(ROWS, dtype=jnp.int32), GRID_W)
    pos_c = jnp.tile(jnp.arange(GRID_W, dtype=jnp.int32), ROWS)
    src_l = jax.nn.silu(c)
    src_c = jax.nn.silu(c_ctx)[None]
    x_l, x_c = x, ctx
    for i in range(DEPTH):
        need_ctx = i < DEPTH - 1
        kind = i % N_MIXERS
        j = i // N_MIXERS
        m_l = (src_l @ ada_w[i] + ada_b[i])[:, None, :]
        m_c = (src_c @ ada_w[i] + ada_b[i])[:, None, :]
        sh1_l, sc1_l, gt1_l, sh2_l, sc2_l, gt2_l = jnp.split(m_l, 6, axis=-1)
        sh1_c, sc1_c, gt1_c, sh2_c, sc2_c, gt2_c = jnp.split(m_c, 6, axis=-1)
        h_l = modulate(rmsnorm(x_l, norm_g[i, 0]), sh1_l, sc1_l)
        h_c = modulate(rmsnorm(x_c, norm_g[i, 0]), sh1_c, sc1_c)
        if kind == 0:
            o_l, o_c = windowed_gqa(h_l, h_c, attn_w_qkv[j], attn_sink[j], attn_w_o[j], pos_r, pos_c, need_ctx)
        elif kind == 1:
            o_l, o_c = fourier_mix(h_l, h_c, fnet_w_o[j], need_ctx)
        else:
            o_l, o_c = mla(h_l, h_c, mla_w_in[j], mla_g_q[j], mla_g_kv[j], mla_w_uq[j], mla_w_ukv[j],
                           mla_w_o[j], pos_r, pos_c, need_ctx)
        x_l = x_l + gt1_l * o_l
        h_l = modulate(rmsnorm(x_l, norm_g[i, 1]), sh2_l, sc2_l)
        if need_ctx:
            x_c = x_c + gt1_c * o_c
            h_c = modulate(rmsnorm(x_c, norm_g[i, 1]), sh2_c, sc2_c)
            tokens = jnp.concatenate([h_l.reshape(B * S, D), h_c.reshape(B * C, D)], axis=0)
            y = moe(tokens, router_w, router_b, moe_w_gate[i], moe_w_up[i], moe_w_down[i])
            x_l = x_l + gt2_l * y[:B * S].reshape(B, S, D)
            x_c = x_c + gt2_c * y[B * S:].reshape(B, C, D)
        else:
            y = moe(h_l.reshape(B * S, D), router_w, router_b, moe_w_gate[i], moe_w_up[i], moe_w_down[i])
            x_l = x_l + gt2_l * y.reshape(B, S, D)
    return rmsnorm(x_l, final_g)
```

```python
import functools
import math

import jax
import jax.numpy as jnp
import numpy as np
from jax import lax
from jax.experimental import pallas as pl
from jax.experimental.pallas import tpu as pltpu

D_MODEL = 2048
BATCH = 2
SEQ = 4096
DEPTH = 4
GRID_W = 64
CTX_LEN = 256
N_MIXERS = 3

BLK = 128
A_HQ = 32
A_HKV = 4
A_GROUPS = A_HQ // A_HKV
A_DH = 64
A_SCALE = A_DH ** -0.5

F_GROUPS = 8
F_GROUP_DIM = D_MODEL // F_GROUPS

M_H = 16
M_QR = 512
M_KVR = 512
M_NOPE = 128
M_ROPE = 64
M_V = 128
M_SCALE = (M_NOPE + M_ROPE) ** -0.5

N_EXPERTS = 16
N_EXPERT_GROUPS = 4
EXPERTS_PER_GROUP = N_EXPERTS // N_EXPERT_GROUPS
D_FF = 1408

ROPE_THETA = 10000.0
NORM_EPS = 1e-6
NEG_INF = -1e30

N_LAT = BATCH * SEQ
N_CTX = BATCH * CTX_LEN
N_TOK = N_LAT + N_CTX

LANES = 128
TM = 512
ROW_BLK = 256
ITEM_ROWS = 1536
N_ITEMS = N_EXPERTS + (2 * N_TOK) // ITEM_ROWS
FF_TILE = 128
DOWN_TILE = 256
VMEM_LIMIT = 56 * 1024 * 1024

F32 = jnp.float32
BF16 = jnp.bfloat16


def _cparams(sem):
    return pltpu.CompilerParams(dimension_semantics=sem, vmem_limit_bytes=VMEM_LIMIT)


def _row_group(i, n_lat_tiles, tiles_per_batch):
    return jnp.where(i < n_lat_tiles, i // tiles_per_batch, BATCH)


def _ada_kernel(src_ref, w_ref, b_ref, o_ref):
    s = src_ref[...]
    s = (s * jax.nn.sigmoid(s)).astype(BF16)
    acc = jnp.dot(s, w_ref[...].astype(BF16), preferred_element_type=F32)
    o_ref[...] = acc + b_ref[...]


def _ada_all(src8, ada_w, ada_b):
    tn = 1024
    n6 = ada_w.shape[-1]
    return pl.pallas_call(
        _ada_kernel,
        grid=(DEPTH, n6 // tn),
        in_specs=[
            pl.BlockSpec((8, D_MODEL), lambda l, j: (0, 0)),
            pl.BlockSpec((None, D_MODEL, tn), lambda l, j: (l, 0, j)),
            pl.BlockSpec((None, 1, tn), lambda l, j: (l, 0, j)),
        ],
        out_specs=pl.BlockSpec((None, 8, tn), lambda l, j: (l, 0, j)),
        out_shape=jax.ShapeDtypeStruct((DEPTH, 8, n6), F32),
        compiler_params=_cparams(("arbitrary", "arbitrary")),
        name="ada_ln",
    )(src8, ada_w, ada_b.reshape(DEPTH, 1, n6))


def _mod_spec(layer, which, n_lat_tiles, tiles_per_batch, ngrid):
    def imap(*idx):
        return (layer, _row_group(idx[0], n_lat_tiles, tiles_per_batch), which, 0, 0)
    del ngrid
    return pl.BlockSpec((None, None, None, 1, D_MODEL), imap)


def _norm_mod(x, g, sh, sc):
    y = x * lax.rsqrt(jnp.mean(x * x, axis=-1, keepdims=True) + NORM_EPS) * g
    return y * (1.0 + sc) + sh


def _norm1_kernel(x_ref, g_ref, sh_ref, sc_ref, o_ref):
    o_ref[...] = _norm_mod(x_ref[...], g_ref[...], sh_ref[...], sc_ref[...]).astype(o_ref.dtype)


def _norm1(x, norm_g4, mod5, layer):
    n = x.shape[0]
    nlt, tpb = N_LAT // TM, SEQ // TM
    return pl.pallas_call(
        _norm1_kernel,
        grid=(n // TM,),
        in_specs=[
            pl.BlockSpec((TM, D_MODEL), lambda i: (i, 0)),
            pl.BlockSpec((None, None, 1, D_MODEL), lambda i: (layer, 0, 0, 0)),
            _mod_spec(layer, 0, nlt, tpb, 1),
            _mod_spec(layer, 1, nlt, tpb, 1),
        ],
        out_specs=pl.BlockSpec((TM, D_MODEL), lambda i: (i, 0)),
        out_shape=jax.ShapeDtypeStruct((n, D_MODEL), BF16),
        compiler_params=_cparams(("arbitrary",)),
        name="norm1",
    )(x, norm_g4, mod5, mod5)


def _norm2_kernel(x_ref, g_ref, sh_ref, sc_ref, rw_ref, hp_ref, lg_ref):
    h = _norm_mod(x_ref[...], g_ref[...], sh_ref[...], sc_ref[...])
    lg_ref[...] = lax.dot_general(rw_ref[...], h, (((1,), (1,)), ((), ())),
                                  precision=lax.Precision.HIGHEST, preferred_element_type=F32)
    bits = pltpu.bitcast(h.astype(BF16).astype(F32), jnp.uint32)
    half = D_MODEL // 2
    hp_ref[...] = (bits[:, :half] >> 16) | (bits[:, half:] & jnp.uint32(0xFFFF0000))


def _norm2(x, norm_g4, mod5, router_wt, layer):
    n = x.shape[0]
    nlt, tpb = N_LAT // TM, SEQ // TM
    return pl.pallas_call(
        _norm2_kernel,
        grid=(n // TM,),
        in_specs=[
            pl.BlockSpec((TM, D_MODEL), lambda i: (i, 0)),
            pl.BlockSpec((None, None, 1, D_MODEL), lambda i: (layer, 1, 0, 0)),
            _mod_spec(layer, 3, nlt, tpb, 1),
            _mod_spec(layer, 4, nlt, tpb, 1),
            pl.BlockSpec((N_EXPERTS, D_MODEL), lambda i: (0, 0)),
        ],
        out_specs=[
            pl.BlockSpec((TM, D_MODEL // 2), lambda i: (i, 0)),
            pl.BlockSpec((N_EXPERTS, TM), lambda i: (0, i)),
        ],
        out_shape=[
            jax.ShapeDtypeStruct((n, D_MODEL // 2), jnp.uint32),
            jax.ShapeDtypeStruct((N_EXPERTS, n), F32),
        ],
        compiler_params=_cparams(("arbitrary",)),
        name="norm2_router_logits",
    )(x, norm_g4, mod5, mod5, router_wt)


def _mm_kernel(*refs, nk, n_extra, epilogue):
    a_ref, w_ref = refs[0], refs[1]
    extra = refs[2:2 + n_extra]
    o_ref = refs[2 + n_extra]
    part = jnp.dot(a_ref[...].astype(BF16), w_ref[...].astype(BF16), preferred_element_type=F32)
    if nk == 1:
        epilogue(part, o_ref, *extra)
        return
    acc_ref = refs[3 + n_extra]
    k = pl.program_id(2)

    @pl.when(k == 0)
    def _():
        acc_ref[...] = part

    @pl.when(k > 0)
    def _():
        acc_ref[...] += part

    @pl.when(k == nk - 1)
    def _():
        epilogue(acc_ref[...], o_ref, *extra)


def _matmul(a, a_spec, w, w_spec, extras, out_shape, out_spec, grid, epilogue, tm, tn, name):
    nk = grid[2]
    kern = functools.partial(_mm_kernel, nk=nk, n_extra=len(extras), epilogue=epilogue)
    scratch = [pltpu.VMEM((tm, tn), F32)] if nk > 1 else []
    return pl.pallas_call(
        kern,
        grid=grid,
        in_specs=[a_spec, w_spec] + [s for _, s in extras],
        out_specs=out_spec,
        out_shape=out_shape,
        scratch_shapes=scratch,
        compiler_params=_cparams(("arbitrary", "arbitrary", "arbitrary")),
        name=name,
    )(a, w, *[x for x, _ in extras])


def _epi_plain(acc, o_ref):
    o_ref[...] = acc.astype(o_ref.dtype)


def _epi_resid_gate(acc, o_ref, x_ref, gate_ref):
    o_ref[...] = x_ref[...] + gate_ref[...] * acc


def _epi_rmsnorm(acc, o_ref, g_ref):
    y = acc * lax.rsqrt(jnp.mean(acc * acc, axis=-1, keepdims=True) + NORM_EPS) * g_ref[...]
    o_ref[...] = y.astype(o_ref.dtype)


def _rope_rotate(acc, cos_ref, sin_ref):
    tn = acc.shape[1]
    reps = tn // LANES
    cos = jnp.concatenate([cos_ref[...]] * reps, axis=1) if reps > 1 else cos_ref[...]
    sin = jnp.concatenate([sin_ref[...]] * reps, axis=1) if reps > 1 else sin_ref[...]
    up = pltpu.roll(acc, tn - 16, 1)
    dn = pltpu.roll(acc, 16, 1)
    lane = lax.broadcasted_iota(jnp.int32, acc.shape, 1)
    rot = jnp.where((lane % 32) < 16, up, dn)
    return acc * cos + rot * sin


def _epi_rope(acc, o_ref, cos_ref, sin_ref, *, rope_cols, scale_cols, scale):
    tn = acc.shape[1]
    col = pl.program_id(0) * tn + lax.broadcasted_iota(jnp.int32, acc.shape, 1)
    r = _rope_rotate(acc, cos_ref, sin_ref)
    r = jnp.where((col >= rope_cols[0]) & (col < rope_cols[1]), r, acc)
    if scale_cols:
        r = jnp.where(col < scale_cols, r * scale, r)
    o_ref[...] = r.astype(o_ref.dtype)


def _rope_tables(tm):
    d_axis = A_DH // 2
    inv_freq = ROPE_THETA ** (-jnp.arange(0, d_axis, 2, dtype=F32) / d_axis)
    t = jnp.arange(SEQ, dtype=jnp.int32)
    ang_r = (t // GRID_W).astype(F32)[:, None] * inv_freq
    ang_c = (t % GRID_W).astype(F32)[:, None] * inv_freq
    ang = jnp.concatenate([ang_r, ang_r, ang_c, ang_c] * 2, axis=1)
    sign = jnp.tile(jnp.concatenate([-jnp.ones(16, F32), jnp.ones(16, F32)]), 4)
    cos = jnp.concatenate([jnp.cos(ang), jnp.ones((tm, LANES), F32)], axis=0)
    sin = jnp.concatenate([jnp.sin(ang) * sign, jnp.zeros((tm, LANES), F32)], axis=0)
    return cos, sin


def _rope_spec(tm):
    nlt, tpb = N_LAT // tm, SEQ // tm
    return pl.BlockSpec((tm, LANES), lambda j, i, k: (jnp.where(i < nlt, i % tpb, tpb), 0))


def _proj(a, a_col, w, layer, n_out, tn, epilogue, extras, out_dtype, name):
    n = a.shape[0]
    kdim = w.shape[-2]
    grid = (n_out // tn, n // TM, 1)
    if w.ndim == 3:
        w_spec = pl.BlockSpec((None, kdim, tn), lambda j, i, k: (layer, 0, j))
    else:
        w_spec = pl.BlockSpec((kdim, tn), lambda j, i, k: (0, j))
    return _matmul(
        a, pl.BlockSpec((TM, kdim), lambda j, i, k: (i, a_col)), w, w_spec, extras,
        jax.ShapeDtypeStruct((n, n_out), out_dtype), pl.BlockSpec((TM, tn), lambda j, i, k: (i, j)),
        grid, epilogue, TM, tn, name)


def _out_proj_resid(o, w3, layer_w, x, mod5, layer, name):
    nlt, tpb = N_LAT // TM, SEQ // TM
    tn = 1024
    gate_spec = pl.BlockSpec(
        (None, None, None, 1, tn),
        lambda j, i, k: (layer, _row_group(i, nlt, tpb), 2, 0, j))
    extras = [(x, pl.BlockSpec((TM, tn), lambda j, i, k: (i, j))), (mod5, gate_spec)]
    return _proj(o, 0, w3, layer_w, D_MODEL, tn, _epi_resid_gate, extras, F32, name)


def _dup_half(t, upper):
    h = t[:, A_DH:] if upper else t[:, :A_DH]
    return jnp.concatenate([h, h], axis=1)


def _gqa_kernel(sink_ref, q_ref, kp_ref, ks_ref, kn_ref, vp_ref, vs_ref, vn_ref, kc_ref, vc_ref, o_ref,
                *, n_lat_blocks, blocks_per_seq):
    n = pl.program_id(0)
    is_lat = n < n_lat_blocks
    nb = n % blocks_per_seq
    iq = lax.broadcasted_iota(jnp.int32, (BLK, 3 * BLK), 0)
    ik = lax.broadcasted_iota(jnp.int32, (BLK, 3 * BLK), 1)
    mask = (ik >= iq) & (ik <= iq + 2 * BLK)
    mask = mask & ((nb > 0) | (ik >= BLK)) & ((nb < blocks_per_seq - 1) | (ik < 2 * BLK)) & is_lat
    lane = lax.broadcasted_iota(jnp.int32, (BLK, LANES), 1)
    lo_half = lane < A_DH
    keep_lo = lo_half.astype(F32).astype(BF16)
    keep_hi = (1.0 - lo_half.astype(F32)).astype(BF16)
    n_pairs = A_GROUPS // 2
    for hk in range(A_HKV):
        c0 = (hk // 2) * LANES
        upper = (hk % 2) == 1
        k_all = jnp.concatenate([
            _dup_half(kp_ref[:, c0:c0 + LANES], upper), _dup_half(ks_ref[:, c0:c0 + LANES], upper),
            _dup_half(kn_ref[:, c0:c0 + LANES], upper), _dup_half(kc_ref[:, c0:c0 + LANES], upper)], axis=0)
        v_all = jnp.concatenate([
            _dup_half(vp_ref[:, c0:c0 + LANES], upper), _dup_half(vs_ref[:, c0:c0 + LANES], upper),
            _dup_half(vn_ref[:, c0:c0 + LANES], upper), _dup_half(vc_ref[:, c0:c0 + LANES], upper)], axis=0)
        lhs = []
        for p in range(n_pairs):
            qp = q_ref[:, hk * A_GROUPS * A_DH + p * LANES: hk * A_GROUPS * A_DH + (p + 1) * LANES]
            lhs.append(qp * keep_lo)
            lhs.append(qp * keep_hi)
        lhs = jnp.concatenate(lhs, axis=0)
        s = lax.dot_general(lhs, k_all, (((1,), (1,)), ((), ())), preferred_element_type=F32)
        outs = []
        for g in range(A_GROUPS):
            sg = s[g * BLK:(g + 1) * BLK]
            s_loc = jnp.where(mask, sg[:, :3 * BLK], NEG_INF)
            s_ctx = sg[:, 3 * BLK:]
            sink = sink_ref[hk * A_GROUPS + g]
            m = jnp.maximum(jnp.maximum(jnp.max(s_loc, axis=-1, keepdims=True),
                                        jnp.max(s_ctx, axis=-1, keepdims=True)), sink)
            p_loc = jnp.exp(s_loc - m)
            p_ctx = jnp.exp(s_ctx - m)
            denom = (jnp.sum(p_loc, axis=-1, keepdims=True) + jnp.sum(p_ctx, axis=-1, keepdims=True)
                     + jnp.exp(sink - m))
            pg = jnp.concatenate([p_loc, p_ctx], axis=1).astype(BF16)
            og = jnp.dot(pg, v_all, preferred_element_type=F32)
            outs.append(og / denom)
        for p in range(n_pairs):
            col = hk * A_GROUPS * A_DH + p * LANES
            o_ref[:, col:col + LANES] = jnp.where(lo_half, outs[2 * p], outs[2 * p + 1]).astype(o_ref.dtype)


def _gqa_attention(qkv, sink, need_ctx):
    bps = SEQ // BLK
    nlb = BATCH * bps
    n_ctx_blocks = N_CTX // BLK if need_ctx else 0
    nq = nlb + n_ctx_blocks
    kcol, vcol = (A_HQ * A_DH) // 256, (A_HQ * A_DH) // 256 + 1
    cpb = CTX_LEN // BLK

    def batch_of(n):
        return jnp.where(n < nlb, n // bps, (n - nlb) // cpb)

    def loc_map(delta, col):
        def imap(n, s):
            nb = n % bps
            blk = jnp.clip(nb + delta, 0, bps - 1)
            return (jnp.where(n < nlb, (n // bps) * bps + blk, 0), col)
        return imap

    def ctx_map(col):
        return lambda n, s: (N_LAT // CTX_LEN + batch_of(n), col)

    kv_blk = (BLK, 256)
    grid_spec = pltpu.PrefetchScalarGridSpec(
        num_scalar_prefetch=1,
        grid=(nq,),
        in_specs=[
            pl.BlockSpec((BLK, A_HQ * A_DH), lambda n, s: (n, 0)),
            pl.BlockSpec(kv_blk, loc_map(-1, kcol)), pl.BlockSpec(kv_blk, loc_map(0, kcol)),
            pl.BlockSpec(kv_blk, loc_map(1, kcol)),
            pl.BlockSpec(kv_blk, loc_map(-1, vcol)), pl.BlockSpec(kv_blk, loc_map(0, vcol)),
            pl.BlockSpec(kv_blk, loc_map(1, vcol)),
            pl.BlockSpec((CTX_LEN, 256), ctx_map(kcol)), pl.BlockSpec((CTX_LEN, 256), ctx_map(vcol)),
        ],
        out_specs=pl.BlockSpec((BLK, A_HQ * A_DH), lambda n, s: (n, 0)),
    )
    kern = functools.partial(_gqa_kernel, n_lat_blocks=nlb, blocks_per_seq=bps)
    return pl.pallas_call(
        kern,
        grid_spec=grid_spec,
        out_shape=jax.ShapeDtypeStruct((nq * BLK, A_HQ * A_DH), BF16),
        compiler_params=_cparams(("arbitrary",)),
        name="gqa_window_attn",
    )(sink, qkv, qkv, qkv, qkv, qkv, qkv, qkv, qkv, qkv)


MLA_TQ = 256
MLA_TK = 512


def _mla_kernel(*refs, n_lat_keys):
    if n_lat_keys:
        (qn_ref, qr_ref, kn_ref, v_ref, kr_ref, knc_ref, vc_ref, krc_ref, o_ref, kcat, vcat) = refs
    else:
        (qn_ref, qr_ref, knc_ref, vc_ref, krc_ref, o_ref, kcat, vcat) = refs
    qi = pl.program_id(2)
    nk = n_lat_keys + CTX_LEN

    @pl.when(qi == 0)
    def _():
        if n_lat_keys:
            kcat[0:n_lat_keys, 0:LANES] = kn_ref[...]
            kcat[0:n_lat_keys, LANES:2 * LANES] = kr_ref[...]
            vcat[0:n_lat_keys, :] = v_ref[...]
        kcat[n_lat_keys:nk, 0:LANES] = knc_ref[...]
        kcat[n_lat_keys:nk, LANES:2 * LANES] = krc_ref[...]
        vcat[n_lat_keys:nk, :] = vc_ref[...]

    q = jnp.concatenate([qn_ref[...], qr_ref[...]], axis=1)
    tq = q.shape[0]

    def chunk(k0, size, carry):
        m, l, acc = carry
        kc = kcat[pl.ds(k0, size), :]
        vc = vcat[pl.ds(k0, size), :]
        s = lax.dot_general(q, kc, (((1,), (1,)), ((), ())), preferred_element_type=F32)
        m_new = jnp.maximum(m, jnp.max(s, axis=-1, keepdims=True))
        alpha = jnp.exp(m - m_new)
        p = jnp.exp(s - m_new)
        l = alpha * l + jnp.sum(p, axis=-1, keepdims=True)
        acc = alpha * acc + jnp.dot(p.astype(BF16), vc, preferred_element_type=F32)
        return m_new, l, acc

    carry = (jnp.full((tq, 1), NEG_INF, F32), jnp.zeros((tq, 1), F32), jnp.zeros((tq, M_V), F32))
    if n_lat_keys:
        carry = lax.fori_loop(
            0, n_lat_keys // MLA_TK,
            lambda c, cr: chunk(pl.multiple_of(c * MLA_TK, MLA_TK), MLA_TK, cr), carry)
    carry = chunk(n_lat_keys, CTX_LEN, carry)
    _, l, acc = carry
    o_ref[...] = (acc / l).astype(o_ref.dtype)


def _mla_attention(qcat, kv, kr, latent):
    hq = M_H
    ctx_row_blk = N_LAT // CTX_LEN
    if latent:
        tq, nqt, n_lat_keys = MLA_TQ, SEQ // MLA_TQ, SEQ
        row0 = lambda b, qi: b * nqt + qi
        n_out = N_LAT
    else:
        tq, nqt, n_lat_keys = CTX_LEN, 1, 0
        row0 = lambda b, qi: ctx_row_blk + b
        n_out = N_CTX
    in_specs = [
        pl.BlockSpec((tq, LANES), lambda b, h, qi: (row0(b, qi), h)),
        pl.BlockSpec((tq, LANES), lambda b, h, qi: (row0(b, qi), hq + h)),
    ]
    args = [qcat, qcat]
    if latent:
        in_specs += [
            pl.BlockSpec((SEQ, LANES), lambda b, h, qi: (b, 2 * h)),
            pl.BlockSpec((SEQ, LANES), lambda b, h, qi: (b, 2 * h + 1)),
            pl.BlockSpec((SEQ, LANES), lambda b, h, qi: (b, 0)),
        ]
        args += [kv, kv, kr]
    in_specs += [
        pl.BlockSpec((CTX_LEN, LANES), lambda b, h, qi: (ctx_row_blk + b, 2 * h)),
        pl.BlockSpec((CTX_LEN, LANES), lambda b, h, qi: (ctx_row_blk + b, 2 * h + 1)),
        pl.BlockSpec((CTX_LEN, LANES), lambda b, h, qi: (ctx_row_blk + b, 0)),
    ]
    args += [kv, kv, kr]
    if latent:
        out_spec = pl.BlockSpec((tq, LANES), lambda b, h, qi: (b * nqt + qi, h))
    else:
        out_spec = pl.BlockSpec((tq, LANES), lambda b, h, qi: (b, h))
    nk = n_lat_keys + CTX_LEN
    return pl.pallas_call(
        functools.partial(_mla_kernel, n_lat_keys=n_lat_keys),
        grid=(BATCH, hq, nqt),
        in_specs=in_specs,
        out_specs=out_spec,
        out_shape=jax.ShapeDtypeStruct((n_out, hq * M_V), BF16),
        scratch_shapes=[pltpu.VMEM((nk, 2 * LANES), BF16), pltpu.VMEM((nk, M_V), BF16)],
        compiler_params=_cparams(("arbitrary", "arbitrary", "arbitrary")),
        name="mla_attn_lat" if latent else "mla_attn_ctx",
    )(*args)


def _dft_mats(n, scale):
    k = jnp.arange(n, dtype=jnp.int32)
    jk = (k[:, None] * k[None, :]) % n
    ang = jk.astype(F32) * (2.0 * math.pi / n)
    return jnp.cos(ang) * scale, jnp.sin(ang) * scale


def _fnet_stage1_kernel(h_ref, w_ref, z_ref):
    z = jnp.dot(h_ref[...], w_ref[...], preferred_element_type=F32)
    z_ref[0] = z[:, :F_GROUP_DIM].astype(z_ref.dtype)
    z_ref[1] = z[:, F_GROUP_DIM:].astype(z_ref.dtype)


def _fnet_stage1(h1, w1, row_blk0, seq_len, nbatch):
    tm = min(TM, seq_len)
    tiles = seq_len // tm
    return pl.pallas_call(
        _fnet_stage1_kernel,
        grid=(nbatch, tiles, F_GROUPS),
        in_specs=[
            pl.BlockSpec((tm, F_GROUP_DIM), lambda b, i, g: (row_blk0 + b * tiles + i, g)),
            pl.BlockSpec((F_GROUP_DIM, 2 * F_GROUP_DIM), lambda b, i, g: (0, 0)),
        ],
        out_specs=pl.BlockSpec((None, 2, tm, F_GROUP_DIM), lambda b, i, g: (b, 0, i, g)),
        out_shape=jax.ShapeDtypeStruct((nbatch, 2, seq_len, D_MODEL), BF16),
        compiler_params=_cparams(("arbitrary", "arbitrary", "arbitrary")),
        name="fnet_channel_dft",
    )(h1, w1)


def _fnet_stage2_kernel(a_ref, z_ref, o_ref, acc_ref, *, nk):
    k = pl.program_id(2)
    part = jnp.dot(a_ref[...], z_ref[...], preferred_element_type=F32)

    @pl.when(k == 0)
    def _():
        acc_ref[...] = part

    @pl.when(k > 0)
    def _():
        acc_ref[...] += part

    @pl.when(k == nk - 1)
    def _():
        o_ref[...] = acc_ref[...].astype(o_ref.dtype)


def _fnet_stage2(a2, z, seq_len, nbatch):
    tm = min(TM, seq_len)
    tk = min(512, seq_len)
    nk = 2 * seq_len // tk
    z2 = z.reshape(nbatch, 2 * seq_len, D_MODEL)
    return pl.pallas_call(
        functools.partial(_fnet_stage2_kernel, nk=nk),
        grid=(nbatch, seq_len // tm, nk),
        in_specs=[
            pl.BlockSpec((tm, tk), lambda b, i, k: (i, k)),
            pl.BlockSpec((None, tk, D_MODEL), lambda b, i, k: (b, k, 0)),
        ],
        out_specs=pl.BlockSpec((None, tm, D_MODEL), lambda b, i, k: (b, i, 0)),
        out_shape=jax.ShapeDtypeStruct((nbatch, seq_len, D_MODEL), BF16),
        scratch_shapes=[pltpu.VMEM((tm, D_MODEL), F32)],
        compiler_params=_cparams(("arbitrary", "arbitrary", "arbitrary")),
        name="fnet_position_dft",
    )(a2, z2)


ROUTER_TILE = 512


def _top2_rows(v, rowf):
    big = float(v.shape[0])
    m1 = jnp.max(v, axis=0, keepdims=True)
    i1 = jnp.min(jnp.where(v == m1, rowf, big), axis=0, keepdims=True)
    v2 = jnp.where(rowf == i1, NEG_INF, v)
    m2 = jnp.max(v2, axis=0, keepdims=True)
    i2 = jnp.min(jnp.where(v2 == m2, rowf, big), axis=0, keepdims=True)
    return m1, i1, m2, i2


def _router_kernel(lg_ref, b_ref, idx_ref, w_ref, cnt_ref, carry_ref):
    step = pl.program_id(0)
    t = lg_ref.shape[1]

    @pl.when(step == 0)
    def _():
        carry_ref[...] = jnp.zeros_like(carry_ref)

    scores = jax.nn.sigmoid(lg_ref[...])
    sel = scores + b_ref[...][:, 0:1]
    row = lax.broadcasted_iota(jnp.int32, (N_EXPERTS, t), 0)
    rowf = row.astype(F32)
    grp = row // EXPERTS_PER_GROUP
    best = jnp.zeros((1, t), jnp.int32)
    best_v = None
    for g in range(N_EXPERT_GROUPS):
        m1, _, m2, _ = _top2_rows(jnp.where(grp == g, sel, NEG_INF), rowf)
        gs = m1 + m2
        if g == 0:
            best_v = gs
        else:
            upd = gs > best_v
            best = jnp.where(upd, g, best)
            best_v = jnp.where(upd, gs, best_v)
    _, e0, _, e1 = _top2_rows(jnp.where(grp == best, sel, NEG_INF), rowf)
    oh0 = rowf == e0
    oh1 = rowf == e1
    s0 = jnp.sum(jnp.where(oh0, scores, 0.0), axis=0, keepdims=True)
    s1 = jnp.sum(jnp.where(oh1, scores, 0.0), axis=0, keepdims=True)
    tot = s0 + s1
    cnt = jnp.where(oh0 | oh1, 1.0, 0.0)
    a = lax.broadcasted_iota(jnp.int32, (t, t), 0)
    bcol = lax.broadcasted_iota(jnp.int32, (t, t), 1)
    tri = jnp.where(a < bcol, 1.0, 0.0).astype(BF16)
    excl = jnp.dot(cnt.astype(BF16), tri, preferred_element_type=F32) + carry_ref[...][:, 0:1]
    r0 = jnp.sum(jnp.where(oh0, excl, 0.0), axis=0, keepdims=True)
    r1 = jnp.sum(jnp.where(oh1, excl, 0.0), axis=0, keepdims=True)
    new_carry = carry_ref[...] + jnp.sum(cnt, axis=1, keepdims=True)
    carry_ref[...] = new_carry
    idx_ref[...] = jnp.zeros(idx_ref.shape, jnp.int32)
    idx_ref[0:1, :] = e0.astype(jnp.int32)
    idx_ref[1:2, :] = e1.astype(jnp.int32)
    idx_ref[2:3, :] = r0.astype(jnp.int32)
    idx_ref[3:4, :] = r1.astype(jnp.int32)
    w_ref[...] = jnp.zeros(w_ref.shape, F32)
    w_ref[0:1, :] = s0 / tot
    w_ref[1:2, :] = s1 / tot
    cnt_ref[...] = new_carry.astype(jnp.int32)


def _router(logits_t, router_b):
    n = logits_t.shape[1]
    t = ROUTER_TILE
    b2 = jnp.broadcast_to(router_b.astype(F32)[:, None], (N_EXPERTS, LANES))
    return pl.pallas_call(
        _router_kernel,
        grid=(n // t,),
        in_specs=[pl.BlockSpec((N_EXPERTS, t), lambda i: (0, i)),
                  pl.BlockSpec((N_EXPERTS, LANES), lambda i: (0, 0))],
        out_specs=[pl.BlockSpec((8, t), lambda i: (0, i)),
                   pl.BlockSpec((8, t), lambda i: (0, i)),
                   pl.BlockSpec((N_EXPERTS, LANES), lambda i: (0, 0))],
        out_shape=[jax.ShapeDtypeStruct((8, n), jnp.int32),
                   jax.ShapeDtypeStruct((8, n), F32),
                   jax.ShapeDtypeStruct((N_EXPERTS, LANES), jnp.int32)],
        scratch_shapes=[pltpu.VMEM((N_EXPERTS, LANES), F32)],
        compiler_params=_cparams(("arbitrary",)),
        name="moe_router",
    )(logits_t, b2)


def _plan_items(counts):
    n_chunks = (counts + ITEM_ROWS - 1) // ITEM_ROWS
    ends = jnp.cumsum(n_chunks)
    starts = ends - n_chunks
    base = starts * ITEM_ROWS
    n_valid = ends[-1]
    item = jnp.arange(N_ITEMS, dtype=jnp.int32)
    item_src = jnp.minimum(item, n_valid - 1)
    expert = jnp.searchsorted(ends, item_src, side="right").astype(jnp.int32)
    rows = jnp.clip(counts[expert] - (item_src - starts[expert]) * ITEM_ROWS, 0, ITEM_ROWS)
    nblk = jnp.where(item < n_valid, (rows + ROW_BLK - 1) // ROW_BLK, 0).astype(jnp.int32)
    return base.astype(jnp.int32), item_src.astype(jnp.int32), expert, nblk


DISPATCH_TILE = 256


def _dispatch_kernel(p0_ref, p1_ref, hp_ref, hs_in_ref, hs_ref, sem):
    del hs_in_ref
    base = pl.program_id(0) * DISPATCH_TILE

    def copies(r):
        t = base + r
        return (pltpu.make_async_copy(hp_ref.at[pl.ds(t, 1)], hs_ref.at[pl.ds(p0_ref[t], 1)], sem),
                pltpu.make_async_copy(hp_ref.at[pl.ds(t, 1)], hs_ref.at[pl.ds(p1_ref[t], 1)], sem))

    def start(r, c):
        a, b = copies(r)
        a.start()
        b.start()
        return c

    def wait(r, c):
        a, b = copies(r)
        a.wait()
        b.wait()
        return c

    lax.fori_loop(0, DISPATCH_TILE, start, 0)
    lax.fori_loop(0, DISPATCH_TILE, wait, 0)


def _dispatch(hp, pos0, pos1):
    n = hp.shape[0]
    hs0 = jnp.zeros((N_ITEMS * ITEM_ROWS, D_MODEL // 2), jnp.uint32)
    grid_spec = pltpu.PrefetchScalarGridSpec(
        num_scalar_prefetch=2,
        grid=(n // DISPATCH_TILE,),
        in_specs=[pl.BlockSpec(memory_space=pl.ANY), pl.BlockSpec(memory_space=pl.ANY)],
        out_specs=pl.BlockSpec(memory_space=pl.ANY),
        scratch_shapes=[pltpu.SemaphoreType.DMA(())],
    )
    return pl.pallas_call(
        _dispatch_kernel,
        grid_spec=grid_spec,
        out_shape=jax.ShapeDtypeStruct(hs0.shape, hs0.dtype),
        input_output_aliases={3: 0},
        compiler_params=pltpu.CompilerParams(dimension_semantics=("arbitrary",), has_side_effects=True),
        name="moe_dispatch",
    )(pos0, pos1, hp, hs0)


def _gate_up_kernel(src_ref, exp_ref, nblk_ref, hs_ref, wg_ref, wu_ref, a_ref, hb_ref, wcat_ref):
    del src_ref, exp_ref
    i = pl.program_id(0)
    f = pl.program_id(1)
    nblk = nblk_ref[i]
    half = D_MODEL // 2

    @pl.when(nblk > 0)
    def _():
        @pl.when(f == 0)
        def _():
            def unpack(r, c):
                rows = pl.ds(pl.multiple_of(r * ROW_BLK, ROW_BLK), ROW_BLK)
                u = hs_ref[rows, :]
                lo = pltpu.bitcast(u << 16, F32)
                hi = pltpu.bitcast(u & jnp.uint32(0xFFFF0000), F32)
                hb_ref[rows, 0:half] = lo.astype(BF16)
                hb_ref[rows, half:D_MODEL] = hi.astype(BF16)
                return c
            lax.fori_loop(0, nblk, unpack, 0)

        wcat_ref[:, 0:FF_TILE] = wg_ref[...].astype(BF16)
        wcat_ref[:, FF_TILE:2 * FF_TILE] = wu_ref[...].astype(BF16)

        for r in range(ITEM_ROWS // ROW_BLK):
            rows = pl.ds(r * ROW_BLK, ROW_BLK)

            @pl.when(r < nblk)
            def _():
                res = jnp.dot(hb_ref[rows, :], wcat_ref[...], preferred_element_type=F32)
                g = res[:, :FF_TILE]
                u = res[:, FF_TILE:]
                a_ref[rows, :] = (g * jax.nn.sigmoid(g) * u).astype(a_ref.dtype)

            @pl.when(r >= nblk)
            def _():
                a_ref[rows, :] = jnp.zeros((ROW_BLK, FF_TILE), a_ref.dtype)


def _gate_up(hs, w_gate, w_up, layer, item_src, expert, nblk):
    nf = D_FF // FF_TILE

    def frozen_f(i, f, nb):
        return jnp.where(nb[i] > 0, f, nf - 1)

    grid_spec = pltpu.PrefetchScalarGridSpec(
        num_scalar_prefetch=3,
        grid=(N_ITEMS, nf),
        in_specs=[
            pl.BlockSpec((ITEM_ROWS, D_MODEL // 2), lambda i, f, src, ex, nb: (src[i], 0)),
            pl.BlockSpec((None, None, D_MODEL, FF_TILE),
                         lambda i, f, src, ex, nb: (layer, ex[i], 0, frozen_f(i, f, nb))),
            pl.BlockSpec((None, None, D_MODEL, FF_TILE),
                         lambda i, f, src, ex, nb: (layer, ex[i], 0, frozen_f(i, f, nb))),
        ],
        out_specs=pl.BlockSpec((ITEM_ROWS, FF_TILE), lambda i, f, src, ex, nb: (src[i], frozen_f(i, f, nb))),
        scratch_shapes=[pltpu.VMEM((ITEM_ROWS, D_MODEL), BF16), pltpu.VMEM((D_MODEL, 2 * FF_TILE), BF16)],
    )
    return pl.pallas_call(
        _gate_up_kernel,
        grid_spec=grid_spec,
        out_shape=jax.ShapeDtypeStruct((N_ITEMS * ITEM_ROWS, D_FF), BF16),
        compiler_params=_cparams(("arbitrary", "arbitrary")),
        name="moe_gate_up",
    )(item_src, expert, nblk, hs, w_gate, w_up)


def _down_kernel(src_ref, exp_ref, nblk_ref, a_ref, wd_ref, y_ref, wb_ref):
    del src_ref, exp_ref
    i = pl.program_id(0)
    nblk = nblk_ref[i]

    @pl.when(nblk > 0)
    def _():
        wb_ref[...] = wd_ref[...].astype(BF16)
        for r in range(ITEM_ROWS // ROW_BLK):
            rows = pl.ds(r * ROW_BLK, ROW_BLK)

            @pl.when(r < nblk)
            def _():
                y_ref[rows, :] = jnp.dot(a_ref[rows, :], wb_ref[...], preferred_element_type=F32)

            @pl.when(r >= nblk)
            def _():
                y_ref[rows, :] = jnp.zeros((ROW_BLK, DOWN_TILE), y_ref.dtype)


def _down(a, w_down, layer, item_src, expert, nblk):
    nj = D_MODEL // DOWN_TILE

    def frozen_j(i, j, nb):
        return jnp.where(nb[i] > 0, j, nj - 1)

    grid_spec = pltpu.PrefetchScalarGridSpec(
        num_scalar_prefetch=3,
        grid=(N_ITEMS, nj),
        in_specs=[
            pl.BlockSpec((ITEM_ROWS, D_FF), lambda i, j, src, ex, nb: (src[i], 0)),
            pl.BlockSpec((None, None, D_FF, DOWN_TILE),
                         lambda i, j, src, ex, nb: (layer, ex[i], 0, frozen_j(i, j, nb))),
        ],
        out_specs=pl.BlockSpec((ITEM_ROWS, DOWN_TILE), lambda i, j, src, ex, nb: (src[i], frozen_j(i, j, nb))),
        scratch_shapes=[pltpu.VMEM((D_FF, DOWN_TILE), BF16)],
    )
    return pl.pallas_call(
        _down_kernel,
        grid_spec=grid_spec,
        out_shape=jax.ShapeDtypeStruct((N_ITEMS * ITEM_ROWS, D_MODEL), F32),
        compiler_params=_cparams(("arbitrary", "arbitrary")),
        name="moe_down",
    )(item_src, expert, nblk, a, w_down)


COMBINE_TILE = 256


def _combine_kernel(p0_ref, p1_ref, x_ref, gate_ref, w0_ref, w1_ref, ys_ref, o_ref, y0_ref, y1_ref, sem):
    base = pl.program_id(0) * COMBINE_TILE

    def copies(r):
        t = base + r
        return (pltpu.make_async_copy(ys_ref.at[pl.ds(p0_ref[t], 1)], y0_ref.at[pl.ds(r, 1)], sem),
                pltpu.make_async_copy(ys_ref.at[pl.ds(p1_ref[t], 1)], y1_ref.at[pl.ds(r, 1)], sem))

    def start(r, c):
        a, b = copies(r)
        a.start()
        b.start()
        return c

    def wait(r, c):
        a, b = copies(r)
        a.wait()
        b.wait()
        return c

    lax.fori_loop(0, COMBINE_TILE, start, 0)
    lax.fori_loop(0, COMBINE_TILE, wait, 0)
    y = w0_ref[...] * y0_ref[...] + w1_ref[...] * y1_ref[...]
    o_ref[...] = x_ref[...] + gate_ref[...] * y


def _combine(x, ys, pos0, pos1, w0, w1, mod5, layer):
    n = x.shape[0]
    t = COMBINE_TILE
    nlt, tpb = N_LAT // t, SEQ // t
    grid_spec = pltpu.PrefetchScalarGridSpec(
        num_scalar_prefetch=2,
        grid=(n // t,),
        in_specs=[
            pl.BlockSpec((t, D_MODEL), lambda i, p0, p1: (i, 0)),
            pl.BlockSpec((None, None, None, 1, D_MODEL),
                         lambda i, p0, p1: (layer, _row_group(i, nlt, tpb), 5, 0, 0)),
            pl.BlockSpec((t, 1), lambda i, p0, p1: (i, 0)),
            pl.BlockSpec((t, 1), lambda i, p0, p1: (i, 0)),
            pl.BlockSpec(memory_space=pl.ANY),
        ],
        out_specs=pl.BlockSpec((t, D_MODEL), lambda i, p0, p1: (i, 0)),
        scratch_shapes=[pltpu.VMEM((t, D_MODEL), F32), pltpu.VMEM((t, D_MODEL), F32),
                        pltpu.SemaphoreType.DMA(())],
    )
    return pl.pallas_call(
        _combine_kernel,
        grid_spec=grid_spec,
        out_shape=jax.ShapeDtypeStruct((n, D_MODEL), F32),
        compiler_params=_cparams(("arbitrary",)),
        name="moe_combine",
    )(pos0, pos1, x, mod5, w0, w1, ys)


def _moe_layer(x, norm_g4, mod5, router_wt, router_b, w_gate, w_up, w_down, layer):
    hp, logits_t = _norm2(x, norm_g4, mod5, router_wt, layer)
    idx, wts, counts = _router(logits_t, router_b)
    base, item_src, expert, nblk = _plan_items(counts[:, 0])
    pos0 = base[idx[0]] + idx[2]
    pos1 = base[idx[1]] + idx[3]
    hs = _dispatch(hp, pos0, pos1)
    a = _gate_up(hs, w_gate, w_up, layer, item_src, expert, nblk)
    ys = _down(a, w_down, layer, item_src, expert, nblk)
    return _combine(x, ys, pos0, pos1, wts[0][:, None], wts[1][:, None], mod5, layer)


def _final_norm_kernel(x_ref, g_ref, o_ref):
    x = x_ref[...]
    o_ref[...] = x * lax.rsqrt(jnp.mean(x * x, axis=-1, keepdims=True) + NORM_EPS) * g_ref[...]


def _final_norm(x, g):
    n = x.shape[0]
    return pl.pallas_call(
        _final_norm_kernel,
        grid=(n // TM,),
        in_specs=[pl.BlockSpec((TM, D_MODEL), lambda i: (i, 0)), pl.BlockSpec((1, D_MODEL), lambda i: (0, 0))],
        out_specs=pl.BlockSpec((TM, D_MODEL), lambda i: (i, 0)),
        out_shape=jax.ShapeDtypeStruct((n, D_MODEL), F32),
        compiler_params=_cparams(("arbitrary",)),
        name="final_norm",
    )(x, g.reshape(1, D_MODEL))


def _attn_mixer(h1, x, mod5, layer, j, attn_w_qkv, attn_sink, attn_w_o, rope_cos, rope_sin, need_ctx):
    q_cols = A_HQ * A_DH
    epi = functools.partial(_epi_rope, rope_cols=(0, q_cols + A_HKV * A_DH), scale_cols=q_cols, scale=A_SCALE)
    extras = [(rope_cos, _rope_spec(TM)), (rope_sin, _rope_spec(TM))]
    qkv = _proj(h1, 0, attn_w_qkv, j, q_cols + 2 * A_HKV * A_DH, 512, epi, extras, BF16, "attn_qkv_proj")
    o = _gqa_attention(qkv, attn_sink[j].astype(F32), need_ctx)
    return _out_proj_resid(o, attn_w_o, j, x, mod5, layer, "attn_out_proj")


def _fnet_mixer(h1, x, mod5, layer, j, fnet_w_o, need_ctx):
    cc, sc = _dft_mats(F_GROUP_DIM, F_GROUP_DIM ** -0.5)
    w1 = jnp.concatenate([cc, sc], axis=1).astype(BF16)
    outs = []
    for seq_len, row_blk0, active in ((SEQ, 0, True), (CTX_LEN, N_LAT // min(TM, CTX_LEN), need_ctx)):
        if not active:
            continue
        cl, sl = _dft_mats(seq_len, seq_len ** -0.5)
        a2 = jnp.concatenate([cl, -sl], axis=1).astype(BF16)
        z = _fnet_stage1(h1, w1, row_blk0, seq_len, BATCH)
        y = _fnet_stage2(a2, z, seq_len, BATCH)
        outs.append(y.reshape(BATCH * seq_len, D_MODEL))
    y_all = jnp.concatenate(outs, axis=0) if len(outs) > 1 else outs[0]
    return _out_proj_resid(y_all, fnet_w_o, j, x, mod5, layer, "fnet_out_proj")


def _mla_mixer(h1, x, mod5, layer, j, mla_w_in, mla_g_q, mla_g_kv, mla_w_uq, mla_w_ukv, mla_w_o,
               rope_cos, rope_sin, need_ctx):
    g_cat = jnp.concatenate([mla_g_q[j], mla_g_kv[j]]).reshape(1, M_QR + M_KVR)
    extras = [(g_cat, pl.BlockSpec((1, M_QR), lambda jj, i, k: (0, jj)))]
    cqkv = _proj(h1, 0, mla_w_in, j, M_QR + M_KVR, M_QR, _epi_rmsnorm, extras, BF16, "mla_in_proj")
    w_kr = jnp.pad(mla_w_in[j][:, M_QR + M_KVR:], ((0, 0), (0, LANES - M_ROPE)))
    epi_kr = functools.partial(_epi_rope, rope_cols=(0, LANES), scale_cols=0, scale=1.0)
    rope_extras = [(rope_cos, _rope_spec(TM)), (rope_sin, _rope_spec(TM))]
    kr = _proj(h1, 0, w_kr, 0, LANES, LANES, epi_kr, rope_extras, BF16, "mla_kr_proj")
    w_uq = mla_w_uq[j].reshape(M_QR, M_H, M_NOPE + M_ROPE)
    w_qn = w_uq[:, :, :M_NOPE].reshape(M_QR, M_H * M_NOPE)
    w_qr = jnp.pad(w_uq[:, :, M_NOPE:], ((0, 0), (0, 0), (0, LANES - M_ROPE))).reshape(M_QR, M_H * LANES)
    w_q = jnp.concatenate([w_qn, w_qr], axis=1)
    nope_cols = M_H * M_NOPE
    epi_q = functools.partial(_epi_rope, rope_cols=(nope_cols, nope_cols + M_H * LANES),
                              scale_cols=nope_cols + M_H * LANES, scale=M_SCALE)
    qcat = _proj(cqkv, 0, w_q, 0, nope_cols + M_H * LANES, 1024, epi_q, rope_extras, BF16, "mla_q_proj")
    kv = _proj(cqkv, 1, mla_w_ukv, j, M_H * (M_NOPE + M_V), 1024, _epi_plain, [], BF16, "mla_kv_proj")
    o = _mla_attention(qcat, kv, kr, latent=True)
    if need_ctx:
        o = jnp.concatenate([o, _mla_attention(qcat, kv, kr, latent=False)], axis=0)
    return _out_proj_resid(o, mla_w_o, j, x, mod5, layer, "mla_out_proj")


def kernel(x, c, ctx, c_ctx, ada_w, ada_b, norm_g, final_g, attn_w_qkv, attn_sink, attn_w_o,
           fnet_w_o, mla_w_in, mla_g_q, mla_g_kv, mla_w_uq, mla_w_ukv, mla_w_o,
           router_w, router_b, moe_w_gate, moe_w_up, moe_w_down):
    xt = jnp.concatenate([x.reshape(N_LAT, D_MODEL), ctx.reshape(N_CTX, D_MODEL)], axis=0)
    src8 = jnp.concatenate([c, c_ctx[None], jnp.zeros((8 - BATCH - 1, D_MODEL), F32)], axis=0)
    mod5 = _ada_all(src8, ada_w, ada_b).reshape(DEPTH, 8, 6, 1, D_MODEL)
    norm_g4 = norm_g.reshape(DEPTH, 2, 1, D_MODEL)
    router_wt = router_w.T
    rope_cos, rope_sin = _rope_tables(TM)
    for i in range(DEPTH):
        need_ctx = i < DEPTH - 1
        kind, j = i % N_MIXERS, i // N_MIXERS
        h1 = _norm1(xt, norm_g4, mod5, i)
        if kind == 0:
            x_mix = _attn_mixer(h1, xt, mod5, i, j, attn_w_qkv, attn_sink, attn_w_o, rope_cos, rope_sin, need_ctx)
        elif kind == 1:
            x_mix = _fnet_mixer(h1, xt, mod5, i, j, fnet_w_o, need_ctx)
        else:
            x_mix = _mla_mixer(h1, xt, mod5, i, j, mla_w_in, mla_g_q, mla_g_kv, mla_w_uq, mla_w_ukv, mla_w_o,
                               rope_cos, rope_sin, need_ctx)
        xt = _moe_layer(x_mix, norm_g4, mod5, router_wt, router_b, moe_w_gate, moe_w_up, moe_w_down, i)
    return _final_norm(xt[:N_LAT], final_g).reshape(BATCH, SEQ, D_MODEL)
```

```python
import functools
import math

import jax
import jax.numpy as jnp
import numpy as np
from jax import lax
from jax.experimental import pallas as pl
from jax.experimental.pallas import tpu as pltpu

D_MODEL = 2048
BATCH = 2
SEQ = 4096
DEPTH = 4
GRID_W = 64
CTX_LEN = 256
N_MIXERS = 3

BLK = 128
A_HQ = 32
A_HKV = 4
A_GROUPS = A_HQ // A_HKV
A_DH = 64
A_SCALE = A_DH ** -0.5

F_GROUPS = 8
F_GROUP_DIM = D_MODEL // F_GROUPS

M_H = 16
M_QR = 512
M_KVR = 512
M_NOPE = 128
M_ROPE = 64
M_V = 128
M_SCALE = (M_NOPE + M_ROPE) ** -0.5

N_EXPERTS = 16
N_EXPERT_GROUPS = 4
EXPERTS_PER_GROUP = N_EXPERTS // N_EXPERT_GROUPS
D_FF = 1408

ROPE_THETA = 10000.0
NORM_EPS = 1e-6
NEG_INF = -1e30

N_LAT = BATCH * SEQ
N_CTX = BATCH * CTX_LEN
N_TOK = N_LAT + N_CTX

LANES = 128
TM = 512
TM_PROJ = 1024
ROW_BLK = 256
ITEM_ROWS = 1536
N_ITEMS = N_EXPERTS + (2 * N_TOK) // ITEM_ROWS
FF_TILE = 128
DOWN_TILE = 256
VMEM_LIMIT = 56 * 1024 * 1024

F32 = jnp.float32
BF16 = jnp.bfloat16


def _cparams(sem):
    return pltpu.CompilerParams(dimension_semantics=sem, vmem_limit_bytes=VMEM_LIMIT)


def _row_group(i, n_lat_tiles, tiles_per_batch):
    return jnp.where(i < n_lat_tiles, i // tiles_per_batch, BATCH)


def _ada_kernel(src_ref, w_ref, b_ref, o_ref):
    s = src_ref[...]
    s = (s * jax.nn.sigmoid(s)).astype(BF16)
    acc = jnp.dot(s, w_ref[...].astype(BF16), preferred_element_type=F32)
    o_ref[...] = acc + b_ref[...]


def _ada_all(src8, ada_w, ada_b):
    tn = 1024
    n6 = ada_w.shape[-1]
    return pl.pallas_call(
        _ada_kernel,
        grid=(DEPTH, n6 // tn),
        in_specs=[
            pl.BlockSpec((8, D_MODEL), lambda l, j: (0, 0)),
            pl.BlockSpec((None, D_MODEL, tn), lambda l, j: (l, 0, j)),
            pl.BlockSpec((None, 1, tn), lambda l, j: (l, 0, j)),
        ],
        out_specs=pl.BlockSpec((None, 8, tn), lambda l, j: (l, 0, j)),
        out_shape=jax.ShapeDtypeStruct((DEPTH, 8, n6), F32),
        compiler_params=_cparams(("arbitrary", "arbitrary")),
        name="ada_ln",
    )(src8, ada_w, ada_b.reshape(DEPTH, 1, n6))


def _mod_spec(layer, which, n_lat_tiles, tiles_per_batch, ngrid):
    def imap(*idx):
        return (layer, _row_group(idx[0], n_lat_tiles, tiles_per_batch), which, 0, 0)
    del ngrid
    return pl.BlockSpec((None, None, None, 1, D_MODEL), imap)


def _norm_mod(x, g, sh, sc):
    y = x * lax.rsqrt(jnp.mean(x * x, axis=-1, keepdims=True) + NORM_EPS) * g
    return y * (1.0 + sc) + sh


def _norm1_kernel(x_ref, g_ref, sh_ref, sc_ref, o_ref):
    o_ref[...] = _norm_mod(x_ref[...], g_ref[...], sh_ref[...], sc_ref[...]).astype(o_ref.dtype)


def _norm1(x, norm_g4, mod5, layer):
    n = x.shape[0]
    nlt, tpb = N_LAT // TM, SEQ // TM
    return pl.pallas_call(
        _norm1_kernel,
        grid=(n // TM,),
        in_specs=[
            pl.BlockSpec((TM, D_MODEL), lambda i: (i, 0)),
            pl.BlockSpec((None, None, 1, D_MODEL), lambda i: (layer, 0, 0, 0)),
            _mod_spec(layer, 0, nlt, tpb, 1),
            _mod_spec(layer, 1, nlt, tpb, 1),
        ],
        out_specs=pl.BlockSpec((TM, D_MODEL), lambda i: (i, 0)),
        out_shape=jax.ShapeDtypeStruct((n, D_MODEL), BF16),
        compiler_params=_cparams(("arbitrary",)),
        name="norm1",
    )(x, norm_g4, mod5, mod5)


def _norm2_kernel(x_ref, g_ref, sh_ref, sc_ref, rw_ref, hp_ref, lg_ref):
    h = _norm_mod(x_ref[...], g_ref[...], sh_ref[...], sc_ref[...])
    lg_ref[...] = lax.dot_general(rw_ref[...], h, (((1,), (1,)), ((), ())),
                                  precision=lax.Precision.HIGHEST, preferred_element_type=F32)
    half = D_MODEL // 2
    hp_ref[...] = pltpu.pack_elementwise([h[:, :half], h[:, half:]], packed_dtype=BF16)


def _norm2(x, norm_g4, mod5, router_wt, layer):
    n = x.shape[0]
    nlt, tpb = N_LAT // TM, SEQ // TM
    return pl.pallas_call(
        _norm2_kernel,
        grid=(n // TM,),
        in_specs=[
            pl.BlockSpec((TM, D_MODEL), lambda i: (i, 0)),
            pl.BlockSpec((None, None, 1, D_MODEL), lambda i: (layer, 1, 0, 0)),
            _mod_spec(layer, 3, nlt, tpb, 1),
            _mod_spec(layer, 4, nlt, tpb, 1),
            pl.BlockSpec((N_EXPERTS, D_MODEL), lambda i: (0, 0)),
        ],
        out_specs=[
            pl.BlockSpec((TM, D_MODEL // 2), lambda i: (i, 0)),
            pl.BlockSpec((N_EXPERTS, TM), lambda i: (0, i)),
        ],
        out_shape=[
            jax.ShapeDtypeStruct((n, D_MODEL // 2), jnp.uint32),
            jax.ShapeDtypeStruct((N_EXPERTS, n), F32),
        ],
        compiler_params=_cparams(("arbitrary",)),
        name="norm2_router_logits",
    )(x, norm_g4, mod5, mod5, router_wt)


def _mm_kernel(*refs, nk, n_extra, epilogue):
    a_ref, w_ref = refs[0], refs[1]
    extra = refs[2:2 + n_extra]
    o_ref = refs[2 + n_extra]
    part = jnp.dot(a_ref[...].astype(BF16), w_ref[...].astype(BF16), preferred_element_type=F32)
    if nk == 1:
        epilogue(part, o_ref, *extra)
        return
    acc_ref = refs[3 + n_extra]
    k = pl.program_id(2)

    @pl.when(k == 0)
    def _():
        acc_ref[...] = part

    @pl.when(k > 0)
    def _():
        acc_ref[...] += part

    @pl.when(k == nk - 1)
    def _():
        epilogue(acc_ref[...], o_ref, *extra)


def _matmul(a, a_spec, w, w_spec, extras, out_shape, out_spec, grid, epilogue, tm, tn, name):
    nk = grid[2]
    kern = functools.partial(_mm_kernel, nk=nk, n_extra=len(extras), epilogue=epilogue)
    scratch = [pltpu.VMEM((tm, tn), F32)] if nk > 1 else []
    return pl.pallas_call(
        kern,
        grid=grid,
        in_specs=[a_spec, w_spec] + [s for _, s in extras],
        out_specs=out_spec,
        out_shape=out_shape,
        scratch_shapes=scratch,
        compiler_params=_cparams(("arbitrary", "arbitrary", "arbitrary")),
        name=name,
    )(a, w, *[x for x, _ in extras])


def _epi_plain(acc, o_ref):
    o_ref[...] = acc.astype(o_ref.dtype)


def _epi_resid_gate(acc, o_ref, x_ref, gate_ref):
    o_ref[...] = x_ref[...] + gate_ref[...] * acc


def _epi_rmsnorm(acc, o_ref, g_ref):
    y = acc * lax.rsqrt(jnp.mean(acc * acc, axis=-1, keepdims=True) + NORM_EPS) * g_ref[...]
    o_ref[...] = y.astype(o_ref.dtype)


def _rope_rotate(acc, cos_ref, sin_ref):
    tn = acc.shape[1]
    reps = tn // LANES
    cos = jnp.concatenate([cos_ref[...]] * reps, axis=1) if reps > 1 else cos_ref[...]
    sin = jnp.concatenate([sin_ref[...]] * reps, axis=1) if reps > 1 else sin_ref[...]
    up = pltpu.roll(acc, tn - 16, 1)
    dn = pltpu.roll(acc, 16, 1)
    lane = lax.broadcasted_iota(jnp.int32, acc.shape, 1)
    rot = jnp.where((lane % 32) < 16, up, dn)
    return acc * cos + rot * sin


def _epi_rope(acc, o_ref, cos_ref, sin_ref, *, rope_cols, scale_cols, scale):
    tn = acc.shape[1]
    col = pl.program_id(0) * tn + lax.broadcasted_iota(jnp.int32, acc.shape, 1)
    r = _rope_rotate(acc, cos_ref, sin_ref)
    r = jnp.where((col >= rope_cols[0]) & (col < rope_cols[1]), r, acc)
    if scale_cols:
        r = jnp.where(col < scale_cols, r * scale, r)
    o_ref[...] = r.astype(o_ref.dtype)


def _rope_tables(tm):
    d_axis = A_DH // 2
    inv_freq = ROPE_THETA ** (-jnp.arange(0, d_axis, 2, dtype=F32) / d_axis)
    t = jnp.arange(SEQ, dtype=jnp.int32)
    ang_r = (t // GRID_W).astype(F32)[:, None] * inv_freq
    ang_c = (t % GRID_W).astype(F32)[:, None] * inv_freq
    ang = jnp.concatenate([ang_r, ang_r, ang_c, ang_c] * 2, axis=1)
    sign = jnp.tile(jnp.concatenate([-jnp.ones(16, F32), jnp.ones(16, F32)]), 4)
    cos = jnp.concatenate([jnp.cos(ang), jnp.ones((tm, LANES), F32)], axis=0)
    sin = jnp.concatenate([jnp.sin(ang) * sign, jnp.zeros((tm, LANES), F32)], axis=0)
    return cos, sin


def _rope_spec(tm):
    nlt, tpb = N_LAT // tm, SEQ // tm
    return pl.BlockSpec((tm, LANES), lambda j, i, k: (jnp.where(i < nlt, i % tpb, tpb), 0))


def _proj(a, a_col, w, layer, n_out, tn, epilogue, extras, out_dtype, name):
    n = a.shape[0]
    kdim = w.shape[-2]
    grid = (n_out // tn, pl.cdiv(n, TM_PROJ), 1)
    if w.ndim == 3:
        w_spec = pl.BlockSpec((None, kdim, tn), lambda j, i, k: (layer, 0, j))
    else:
        w_spec = pl.BlockSpec((kdim, tn), lambda j, i, k: (0, j))
    return _matmul(
        a, pl.BlockSpec((TM_PROJ, kdim), lambda j, i, k: (i, a_col)), w, w_spec, extras,
        jax.ShapeDtypeStruct((n, n_out), out_dtype), pl.BlockSpec((TM_PROJ, tn), lambda j, i, k: (i, j)),
        grid, epilogue, TM_PROJ, tn, name)


def _out_proj_resid(o, w3, layer_w, x, mod5, layer, name):
    nlt, tpb = N_LAT // TM_PROJ, SEQ // TM_PROJ
    tn = 1024
    gate_spec = pl.BlockSpec(
        (None, None, None, 1, tn),
        lambda j, i, k: (layer, _row_group(i, nlt, tpb), 2, 0, j))
    extras = [(x, pl.BlockSpec((TM_PROJ, tn), lambda j, i, k: (i, j))), (mod5, gate_spec)]
    return _proj(o, 0, w3, layer_w, D_MODEL, tn, _epi_resid_gate, extras, F32, name)


def _dup_half(t, upper):
    h = t[:, A_DH:] if upper else t[:, :A_DH]
    return jnp.concatenate([h, h], axis=1)


def _gqa_kernel(sink_ref, q_ref, kp_ref, ks_ref, kn_ref, vp_ref, vs_ref, vn_ref, kc_ref, vc_ref, o_ref,
                *, n_lat_blocks, blocks_per_seq):
    n = pl.program_id(0)
    is_lat = n < n_lat_blocks
    nb = n % blocks_per_seq
    iq = lax.broadcasted_iota(jnp.int32, (BLK, 3 * BLK), 0)
    ik = lax.broadcasted_iota(jnp.int32, (BLK, 3 * BLK), 1)
    mask = (ik >= iq) & (ik <= iq + 2 * BLK)
    mask = mask & ((nb > 0) | (ik >= BLK)) & ((nb < blocks_per_seq - 1) | (ik < 2 * BLK)) & is_lat
    lane = lax.broadcasted_iota(jnp.int32, (BLK, LANES), 1)
    lo_half = lane < A_DH
    keep_lo = lo_half.astype(F32).astype(BF16)
    keep_hi = (1.0 - lo_half.astype(F32)).astype(BF16)
    n_pairs = A_GROUPS // 2
    for hk in range(A_HKV):
        c0 = (hk // 2) * LANES
        upper = (hk % 2) == 1
        k_all = jnp.concatenate([
            _dup_half(kp_ref[:, c0:c0 + LANES], upper), _dup_half(ks_ref[:, c0:c0 + LANES], upper),
            _dup_half(kn_ref[:, c0:c0 + LANES], upper), _dup_half(kc_ref[:, c0:c0 + LANES], upper)], axis=0)
        v_all = jnp.concatenate([
            _dup_half(vp_ref[:, c0:c0 + LANES], upper), _dup_half(vs_ref[:, c0:c0 + LANES], upper),
            _dup_half(vn_ref[:, c0:c0 + LANES], upper), _dup_half(vc_ref[:, c0:c0 + LANES], upper)], axis=0)
        lhs = []
        for p in range(n_pairs):
            qp = q_ref[:, hk * A_GROUPS * A_DH + p * LANES: hk * A_GROUPS * A_DH + (p + 1) * LANES]
            lhs.append(qp * keep_lo)
            lhs.append(qp * keep_hi)
        lhs = jnp.concatenate(lhs, axis=0)
        s = lax.dot_general(lhs, k_all, (((1,), (1,)), ((), ())), preferred_element_type=F32)
        outs = []
        for g in range(A_GROUPS):
            sg = s[g * BLK:(g + 1) * BLK]
            s_loc = jnp.where(mask, sg[:, :3 * BLK], NEG_INF)
            s_ctx = sg[:, 3 * BLK:]
            sink = sink_ref[hk * A_GROUPS + g]
            m = jnp.maximum(jnp.maximum(jnp.max(s_loc, axis=-1, keepdims=True),
                                        jnp.max(s_ctx, axis=-1, keepdims=True)), sink)
            p_loc = jnp.exp(s_loc - m)
            p_ctx = jnp.exp(s_ctx - m)
            denom = (jnp.sum(p_loc, axis=-1, keepdims=True) + jnp.sum(p_ctx, axis=-1, keepdims=True)
                     + jnp.exp(sink - m))
            pg = jnp.concatenate([p_loc, p_ctx], axis=1).astype(BF16)
            og = jnp.dot(pg, v_all, preferred_element_type=F32)
            outs.append(og / denom)
        for p in range(n_pairs):
            col = hk * A_GROUPS * A_DH + p * LANES
            o_ref[:, col:col + LANES] = jnp.where(lo_half, outs[2 * p], outs[2 * p + 1]).astype(o_ref.dtype)


def _gqa_attention(qkv, sink, need_ctx):
    bps = SEQ // BLK
    nlb = BATCH * bps
    n_ctx_blocks = N_CTX // BLK if need_ctx else 0
    nq = nlb + n_ctx_blocks
    kcol, vcol = (A_HQ * A_DH) // 256, (A_HQ * A_DH) // 256 + 1
    cpb = CTX_LEN // BLK

    def batch_of(n):
        return jnp.where(n < nlb, n // bps, (n - nlb) // cpb)

    def loc_map(delta, col):
        def imap(n, s):
            nb = n % bps
            blk = jnp.clip(nb + delta, 0, bps - 1)
            return (jnp.where(n < nlb, (n // bps) * bps + blk, 0), col)
        return imap

    def ctx_map(col):
        return lambda n, s: (N_LAT // CTX_LEN + batch_of(n), col)

    kv_blk = (BLK, 256)
    grid_spec = pltpu.PrefetchScalarGridSpec(
        num_scalar_prefetch=1,
        grid=(nq,),
        in_specs=[
            pl.BlockSpec((BLK, A_HQ * A_DH), lambda n, s: (n, 0)),
            pl.BlockSpec(kv_blk, loc_map(-1, kcol)), pl.BlockSpec(kv_blk, loc_map(0, kcol)),
            pl.BlockSpec(kv_blk, loc_map(1, kcol)),
            pl.BlockSpec(kv_blk, loc_map(-1, vcol)), pl.BlockSpec(kv_blk, loc_map(0, vcol)),
            pl.BlockSpec(kv_blk, loc_map(1, vcol)),
            pl.BlockSpec((CTX_LEN, 256), ctx_map(kcol)), pl.BlockSpec((CTX_LEN, 256), ctx_map(vcol)),
        ],
        out_specs=pl.BlockSpec((BLK, A_HQ * A_DH), lambda n, s: (n, 0)),
    )
    kern = functools.partial(_gqa_kernel, n_lat_blocks=nlb, blocks_per_seq=bps)
    return pl.pallas_call(
        kern,
        grid_spec=grid_spec,
        out_shape=jax.ShapeDtypeStruct((nq * BLK, A_HQ * A_DH), BF16),
        compiler_params=_cparams(("arbitrary",)),
        name="gqa_window_attn",
    )(sink, qkv, qkv, qkv, qkv, qkv, qkv, qkv, qkv, qkv)


MLA_TQ = 256
MLA_TK = 512


def _mla_kernel(*refs, n_lat_keys):
    if n_lat_keys:
        (qn_ref, qr_ref, kn_ref, v_ref, kr_ref, knc_ref, vc_ref, krc_ref, o_ref, kcat, vcat) = refs
    else:
        (qn_ref, qr_ref, knc_ref, vc_ref, krc_ref, o_ref, kcat, vcat) = refs
    qi = pl.program_id(2)
    nk = n_lat_keys + CTX_LEN

    @pl.when(qi == 0)
    def _():
        if n_lat_keys:
            kcat[0:n_lat_keys, 0:LANES] = kn_ref[...]
            kcat[0:n_lat_keys, LANES:2 * LANES] = kr_ref[...]
            vcat[0:n_lat_keys, :] = v_ref[...]
        kcat[n_lat_keys:nk, 0:LANES] = knc_ref[...]
        kcat[n_lat_keys:nk, LANES:2 * LANES] = krc_ref[...]
        vcat[n_lat_keys:nk, :] = vc_ref[...]

    q = jnp.concatenate([qn_ref[...], qr_ref[...]], axis=1)

    def chunk(k0, size):
        kc = kcat[k0:k0 + size, :]
        vc = vcat[k0:k0 + size, :]
        s = lax.dot_general(q, kc, (((1,), (1,)), ((), ())), preferred_element_type=F32)
        m_c = jnp.max(s, axis=-1, keepdims=True)
        p = jnp.exp(s - m_c)
        l_c = jnp.sum(p, axis=-1, keepdims=True)
        o_c = jnp.dot(p.astype(BF16), vc, preferred_element_type=F32)
        return m_c, l_c, o_c

    parts = [chunk(c * MLA_TK, MLA_TK) for c in range(n_lat_keys // MLA_TK)]
    parts.append(chunk(n_lat_keys, CTX_LEN))
    m = parts[0][0]
    for m_c, _, _ in parts[1:]:
        m = jnp.maximum(m, m_c)
    l = None
    acc = None
    for m_c, l_c, o_c in parts:
        w_c = jnp.exp(m_c - m)
        l = w_c * l_c if l is None else l + w_c * l_c
        acc = w_c * o_c if acc is None else acc + w_c * o_c
    o_ref[...] = (acc / l).astype(o_ref.dtype)


def _mla_attention(qcat, kv, kr, latent):
    hq = M_H
    ctx_row_blk = N_LAT // CTX_LEN
    if latent:
        tq, nqt, n_lat_keys = MLA_TQ, SEQ // MLA_TQ, SEQ
        row0 = lambda b, qi: b * nqt + qi
        n_out = N_LAT
    else:
        tq, nqt, n_lat_keys = CTX_LEN, 1, 0
        row0 = lambda b, qi: ctx_row_blk + b
        n_out = N_CTX
    in_specs = [
        pl.BlockSpec((tq, LANES), lambda b, h, qi: (row0(b, qi), h)),
        pl.BlockSpec((tq, LANES), lambda b, h, qi: (row0(b, qi), hq + h)),
    ]
    args = [qcat, qcat]
    if latent:
        in_specs += [
            pl.BlockSpec((SEQ, LANES), lambda b, h, qi: (b, 2 * h)),
            pl.BlockSpec((SEQ, LANES), lambda b, h, qi: (b, 2 * h + 1)),
            pl.BlockSpec((SEQ, LANES), lambda b, h, qi: (b, 0)),
        ]
        args += [kv, kv, kr]
    in_specs += [
        pl.BlockSpec((CTX_LEN, LANES), lambda b, h, qi: (ctx_row_blk + b, 2 * h)),
        pl.BlockSpec((CTX_LEN, LANES), lambda b, h, qi: (ctx_row_blk + b, 2 * h + 1)),
        pl.BlockSpec((CTX_LEN, LANES), lambda b, h, qi: (ctx_row_blk + b, 0)),
    ]
    args += [kv, kv, kr]
    if latent:
        out_spec = pl.BlockSpec((tq, LANES), lambda b, h, qi: (b * nqt + qi, h))
    else:
        out_spec = pl.BlockSpec((tq, LANES), lambda b, h, qi: (b, h))
    nk = n_lat_keys + CTX_LEN
    return pl.pallas_call(
        functools.partial(_mla_kernel, n_lat_keys=n_lat_keys),
        grid=(BATCH, hq, nqt),
        in_specs=in_specs,
        out_specs=out_spec,
        out_shape=jax.ShapeDtypeStruct((n_out, hq * M_V), BF16),
        scratch_shapes=[pltpu.VMEM((nk, 2 * LANES), BF16), pltpu.VMEM((nk, M_V), BF16)],
        compiler_params=_cparams(("arbitrary", "arbitrary", "arbitrary")),
        name="mla_attn_lat" if latent else "mla_attn_ctx",
    )(*args)


def _dft_mats(n, scale):
    sub = 64
    k = jnp.arange(n, dtype=jnp.int32)
    j1 = jnp.arange(n // sub, dtype=jnp.int32) * sub
    j0 = jnp.arange(sub, dtype=jnp.int32)
    a = ((j1[:, None] * k[None, :]) % n).astype(F32) * (2.0 * math.pi / n)
    b = ((j0[:, None] * k[None, :]) % n).astype(F32) * (2.0 * math.pi / n)
    ca, sa = (jnp.cos(a) * scale)[:, None, :], (jnp.sin(a) * scale)[:, None, :]
    cb, sb = jnp.cos(b)[None, :, :], jnp.sin(b)[None, :, :]
    return (ca * cb - sa * sb).reshape(n, n), (sa * cb + ca * sb).reshape(n, n)


def _fnet_stage1_kernel(h_ref, w_ref, z_ref):
    z = jnp.dot(h_ref[...], w_ref[...], preferred_element_type=F32)
    z_ref[0] = z[:, :F_GROUP_DIM].astype(z_ref.dtype)
    z_ref[1] = z[:, F_GROUP_DIM:].astype(z_ref.dtype)


def _fnet_stage1(h1, w1, row_blk0, seq_len, nbatch):
    tm = min(TM, seq_len)
    tiles = seq_len // tm
    return pl.pallas_call(
        _fnet_stage1_kernel,
        grid=(nbatch, tiles, F_GROUPS),
        in_specs=[
            pl.BlockSpec((tm, F_GROUP_DIM), lambda b, i, g: (row_blk0 + b * tiles + i, g)),
            pl.BlockSpec((F_GROUP_DIM, 2 * F_GROUP_DIM), lambda b, i, g: (0, 0)),
        ],
        out_specs=pl.BlockSpec((None, 2, tm, F_GROUP_DIM), lambda b, i, g: (b, 0, i, g)),
        out_shape=jax.ShapeDtypeStruct((nbatch, 2, seq_len, D_MODEL), BF16),
        compiler_params=_cparams(("arbitrary", "arbitrary", "arbitrary")),
        name="fnet_channel_dft",
    )(h1, w1)


def _fnet_stage2_kernel(a_ref, z_ref, o_ref, acc_ref, *, nk):
    k = pl.program_id(2)
    part = jnp.dot(a_ref[...], z_ref[...], preferred_element_type=F32)

    @pl.when(k == 0)
    def _():
        acc_ref[...] = part

    @pl.when(k > 0)
    def _():
        acc_ref[...] += part

    @pl.when(k == nk - 1)
    def _():
        o_ref[...] = acc_ref[...].astype(o_ref.dtype)


def _fnet_stage2(a2, z, seq_len, nbatch):
    tm = min(1024, seq_len)
    tk = min(512, seq_len)
    nk = 2 * seq_len // tk
    z2 = z.reshape(nbatch, 2 * seq_len, D_MODEL)
    return pl.pallas_call(
        functools.partial(_fnet_stage2_kernel, nk=nk),
        grid=(nbatch, seq_len // tm, nk),
        in_specs=[
            pl.BlockSpec((tm, tk), lambda b, i, k: (i, k)),
            pl.BlockSpec((None, tk, D_MODEL), lambda b, i, k: (b, k, 0)),
        ],
        out_specs=pl.BlockSpec((None, tm, D_MODEL), lambda b, i, k: (b, i, 0)),
        out_shape=jax.ShapeDtypeStruct((nbatch, seq_len, D_MODEL), BF16),
        scratch_shapes=[pltpu.VMEM((tm, D_MODEL), F32)],
        compiler_params=_cparams(("arbitrary", "arbitrary", "arbitrary")),
        name="fnet_position_dft",
    )(a2, z2)


ROUTER_TILE = 512


def _top2_rows(v, rowf):
    big = float(v.shape[0])
    m1 = jnp.max(v, axis=0, keepdims=True)
    i1 = jnp.min(jnp.where(v == m1, rowf, big), axis=0, keepdims=True)
    v2 = jnp.where(rowf == i1, NEG_INF, v)
    m2 = jnp.max(v2, axis=0, keepdims=True)
    i2 = jnp.min(jnp.where(v2 == m2, rowf, big), axis=0, keepdims=True)
    return m1, i1, m2, i2


def _router_kernel(lg_ref, b_ref, idx_ref, w_ref, cnt_ref, carry_ref):
    step = pl.program_id(0)
    t = lg_ref.shape[1]

    @pl.when(step == 0)
    def _():
        carry_ref[...] = jnp.zeros_like(carry_ref)

    scores = jax.nn.sigmoid(lg_ref[...])
    sel = scores + b_ref[...][:, 0:1]
    row = lax.broadcasted_iota(jnp.int32, (N_EXPERTS, t), 0)
    rowf = row.astype(F32)
    grp = row // EXPERTS_PER_GROUP
    best = jnp.zeros((1, t), jnp.int32)
    best_v = None
    for g in range(N_EXPERT_GROUPS):
        m1, _, m2, _ = _top2_rows(jnp.where(grp == g, sel, NEG_INF), rowf)
        gs = m1 + m2
        if g == 0:
            best_v = gs
        else:
            upd = gs > best_v
            best = jnp.where(upd, g, best)
            best_v = jnp.where(upd, gs, best_v)
    _, e0, _, e1 = _top2_rows(jnp.where(grp == best, sel, NEG_INF), rowf)
    oh0 = rowf == e0
    oh1 = rowf == e1
    s0 = jnp.sum(jnp.where(oh0, scores, 0.0), axis=0, keepdims=True)
    s1 = jnp.sum(jnp.where(oh1, scores, 0.0), axis=0, keepdims=True)
    tot = s0 + s1
    cnt = jnp.where(oh0 | oh1, 1.0, 0.0)
    a = lax.broadcasted_iota(jnp.int32, (t, t), 0)
    bcol = lax.broadcasted_iota(jnp.int32, (t, t), 1)
    tri = jnp.where(a < bcol, 1.0, 0.0).astype(BF16)
    excl = jnp.dot(cnt.astype(BF16), tri, preferred_element_type=F32) + carry_ref[...][:, 0:1]
    r0 = jnp.sum(jnp.where(oh0, excl, 0.0), axis=0, keepdims=True)
    r1 = jnp.sum(jnp.where(oh1, excl, 0.0), axis=0, keepdims=True)
    new_carry = carry_ref[...] + jnp.sum(cnt, axis=1, keepdims=True)
    carry_ref[...] = new_carry
    idx_ref[...] = jnp.zeros(idx_ref.shape, jnp.int32)
    idx_ref[0:1, :] = e0.astype(jnp.int32)
    idx_ref[1:2, :] = e1.astype(jnp.int32)
    idx_ref[2:3, :] = r0.astype(jnp.int32)
    idx_ref[3:4, :] = r1.astype(jnp.int32)
    w_ref[...] = jnp.zeros(w_ref.shape, F32)
    w_ref[0:1, :] = s0 / tot
    w_ref[1:2, :] = s1 / tot
    cnt_ref[...] = new_carry.astype(jnp.int32)


def _router(logits_t, router_b):
    n = logits_t.shape[1]
    t = ROUTER_TILE
    b2 = jnp.broadcast_to(router_b.astype(F32)[:, None], (N_EXPERTS, LANES))
    return pl.pallas_call(
        _router_kernel,
        grid=(n // t,),
        in_specs=[pl.BlockSpec((N_EXPERTS, t), lambda i: (0, i)),
                  pl.BlockSpec((N_EXPERTS, LANES), lambda i: (0, 0))],
        out_specs=[pl.BlockSpec((8, t), lambda i: (0, i)),
                   pl.BlockSpec((8, t), lambda i: (0, i)),
                   pl.BlockSpec((N_EXPERTS, LANES), lambda i: (0, 0))],
        out_shape=[jax.ShapeDtypeStruct((8, n), jnp.int32),
                   jax.ShapeDtypeStruct((8, n), F32),
                   jax.ShapeDtypeStruct((N_EXPERTS, LANES), jnp.int32)],
        scratch_shapes=[pltpu.VMEM((N_EXPERTS, LANES), F32)],
        compiler_params=_cparams(("arbitrary",)),
        name="moe_router",
    )(logits_t, b2)


def _plan_items(counts):
    n_chunks = (counts + ITEM_ROWS - 1) // ITEM_ROWS
    ends = jnp.cumsum(n_chunks)
    starts = ends - n_chunks
    base = starts * ITEM_ROWS
    n_valid = ends[-1]
    item = jnp.arange(N_ITEMS, dtype=jnp.int32)
    item_src = jnp.minimum(item, n_valid - 1)
    expert = jnp.sum((ends[None, :] <= item_src[:, None]).astype(jnp.int32), axis=1)
    rows = jnp.clip(counts[expert] - (item_src - starts[expert]) * ITEM_ROWS, 0, ITEM_ROWS)
    nblk = jnp.where(item < n_valid, (rows + ROW_BLK - 1) // ROW_BLK, 0).astype(jnp.int32)
    return base.astype(jnp.int32), item_src.astype(jnp.int32), expert, nblk


DISPATCH_TILE = 256


def _dispatch_kernel(p0_ref, p1_ref, hp_ref, hs_in_ref, hs_ref, sem):
    del hs_in_ref
    base = pl.program_id(0) * DISPATCH_TILE

    def copies(r):
        t = base + r
        return (pltpu.make_async_copy(hp_ref.at[pl.ds(r, 1)], hs_ref.at[pl.ds(p0_ref[t], 1)], sem),
                pltpu.make_async_copy(hp_ref.at[pl.ds(r, 1)], hs_ref.at[pl.ds(p1_ref[t], 1)], sem))

    def start(r, c):
        a, b = copies(r)
        a.start()
        b.start()
        return c

    def wait(r, c):
        a, b = copies(r)
        a.wait()
        b.wait()
        return c

    lax.fori_loop(0, DISPATCH_TILE, start, 0, unroll=8)
    lax.fori_loop(0, DISPATCH_TILE, wait, 0, unroll=8)


def _dispatch(hp, pos0, pos1):
    n = hp.shape[0]
    hs0 = jnp.zeros((N_ITEMS * ITEM_ROWS, D_MODEL // 2), jnp.uint32)
    grid_spec = pltpu.PrefetchScalarGridSpec(
        num_scalar_prefetch=2,
        grid=(n // DISPATCH_TILE,),
        in_specs=[pl.BlockSpec((DISPATCH_TILE, D_MODEL // 2), lambda i, p0, p1: (i, 0)),
                  pl.BlockSpec(memory_space=pl.ANY)],
        out_specs=pl.BlockSpec(memory_space=pl.ANY),
        scratch_shapes=[pltpu.SemaphoreType.DMA(())],
    )
    return pl.pallas_call(
        _dispatch_kernel,
        grid_spec=grid_spec,
        out_shape=jax.ShapeDtypeStruct(hs0.shape, hs0.dtype),
        input_output_aliases={3: 0},
        compiler_params=pltpu.CompilerParams(dimension_semantics=("arbitrary",), has_side_effects=True),
        name="moe_dispatch",
    )(pos0, pos1, hp, hs0)


def _gate_up_kernel(src_ref, exp_ref, nblk_ref, hs_ref, wg_ref, wu_ref, a_ref, hb_ref):
    del src_ref, exp_ref
    i = pl.program_id(0)
    f = pl.program_id(1)
    nblk = nblk_ref[i]
    half = D_MODEL // 2

    @pl.when(nblk > 0)
    def _():
        @pl.when(f == 0)
        def _():
            def unpack(r, c):
                rows = pl.ds(pl.multiple_of(r * ROW_BLK, ROW_BLK), ROW_BLK)
                u = hs_ref[rows, :]
                lo = pltpu.unpack_elementwise(u, index=0, packed_dtype=BF16, unpacked_dtype=F32)
                hi = pltpu.unpack_elementwise(u, index=1, packed_dtype=BF16, unpacked_dtype=F32)
                hb_ref[rows, 0:half] = lo.astype(BF16)
                hb_ref[rows, half:D_MODEL] = hi.astype(BF16)
                return c
            lax.fori_loop(0, nblk, unpack, 0)

        for m in range(1, ITEM_ROWS // ROW_BLK + 1):
            rows = m * ROW_BLK

            @pl.when(nblk == m)
            def _():
                w = jnp.concatenate([wg_ref[...].astype(BF16), wu_ref[...].astype(BF16)], axis=1)
                res = jnp.dot(hb_ref[0:rows, :], w, preferred_element_type=F32)
                g = res[:, :FF_TILE]
                u = res[:, FF_TILE:]
                a_ref[0:rows, :] = (g * jax.nn.sigmoid(g) * u).astype(a_ref.dtype)
                if rows < ITEM_ROWS:
                    a_ref[rows:ITEM_ROWS, :] = jnp.zeros((ITEM_ROWS - rows, FF_TILE), a_ref.dtype)


def _gate_up(hs, w_gate, w_up, layer, item_src, expert, nblk):
    nf = D_FF // FF_TILE

    def frozen_f(i, f, nb):
        return jnp.where(nb[i] > 0, f, nf - 1)

    grid_spec = pltpu.PrefetchScalarGridSpec(
        num_scalar_prefetch=3,
        grid=(N_ITEMS, nf),
        in_specs=[
            pl.BlockSpec((ITEM_ROWS, D_MODEL // 2), lambda i, f, src, ex, nb: (src[i], 0)),
            pl.BlockSpec((None, None, D_MODEL, FF_TILE),
                         lambda i, f, src, ex, nb: (layer, ex[i], 0, frozen_f(i, f, nb))),
            pl.BlockSpec((None, None, D_MODEL, FF_TILE),
                         lambda i, f, src, ex, nb: (layer, ex[i], 0, frozen_f(i, f, nb))),
        ],
        out_specs=pl.BlockSpec((ITEM_ROWS, FF_TILE), lambda i, f, src, ex, nb: (src[i], frozen_f(i, f, nb))),
        scratch_shapes=[pltpu.VMEM((ITEM_ROWS, D_MODEL), BF16)],
    )
    return pl.pallas_call(
        _gate_up_kernel,
        grid_spec=grid_spec,
        out_shape=jax.ShapeDtypeStruct((N_ITEMS * ITEM_ROWS, D_FF), BF16),
        compiler_params=_cparams(("arbitrary", "arbitrary")),
        name="moe_gate_up",
    )(item_src, expert, nblk, hs, w_gate, w_up)


def _down_kernel(src_ref, exp_ref, nblk_ref, a_ref, wd_ref, y_ref):
    del src_ref, exp_ref
    i = pl.program_id(0)
    nblk = nblk_ref[i]

    for m in range(1, ITEM_ROWS // ROW_BLK + 1):
        rows = m * ROW_BLK

        @pl.when(nblk == m)
        def _():
            y_ref[0:rows, :] = jnp.dot(a_ref[0:rows, :], wd_ref[...].astype(BF16), preferred_element_type=F32)
            if rows < ITEM_ROWS:
                y_ref[rows:ITEM_ROWS, :] = jnp.zeros((ITEM_ROWS - rows, DOWN_TILE), y_ref.dtype)


def _down(a, w_down, layer, item_src, expert, nblk):
    nj = D_MODEL // DOWN_TILE

    def frozen_j(i, j, nb):
        return jnp.where(nb[i] > 0, j, nj - 1)

    grid_spec = pltpu.PrefetchScalarGridSpec(
        num_scalar_prefetch=3,
        grid=(N_ITEMS, nj),
        in_specs=[
            pl.BlockSpec((ITEM_ROWS, D_FF), lambda i, j, src, ex, nb: (src[i], 0)),
            pl.BlockSpec((None, None, D_FF, DOWN_TILE),
                         lambda i, j, src, ex, nb: (layer, ex[i], 0, frozen_j(i, j, nb))),
        ],
        out_specs=pl.BlockSpec((ITEM_ROWS, DOWN_TILE), lambda i, j, src, ex, nb: (src[i], frozen_j(i, j, nb))),
    )
    return pl.pallas_call(
        _down_kernel,
        grid_spec=grid_spec,
        out_shape=jax.ShapeDtypeStruct((N_ITEMS * ITEM_ROWS, D_MODEL), F32),
        compiler_params=_cparams(("arbitrary", "arbitrary")),
        name="moe_down",
    )(item_src, expert, nblk, a, w_down)


COMBINE_TILE = 256


def _combine_kernel(p0_ref, p1_ref, x_ref, gate_ref, w0_ref, w1_ref, ys_ref, o_ref, y0_ref, y1_ref, sem):
    base = pl.program_id(0) * COMBINE_TILE

    def copies(r):
        t = base + r
        return (pltpu.make_async_copy(ys_ref.at[pl.ds(p0_ref[t], 1)], y0_ref.at[pl.ds(r, 1)], sem),
                pltpu.make_async_copy(ys_ref.at[pl.ds(p1_ref[t], 1)], y1_ref.at[pl.ds(r, 1)], sem))

    def start(r, c):
        a, b = copies(r)
        a.start()
        b.start()
        return c

    def wait(r, c):
        a, b = copies(r)
        a.wait()
        b.wait()
        return c

    lax.fori_loop(0, COMBINE_TILE, start, 0, unroll=8)
    lax.fori_loop(0, COMBINE_TILE, wait, 0, unroll=8)
    y = w0_ref[...] * y0_ref[...] + w1_ref[...] * y1_ref[...]
    o_ref[...] = x_ref[...] + gate_ref[...] * y


def _combine(x, ys, pos0, pos1, w0, w1, mod5, layer):
    n = x.shape[0]
    t = COMBINE_TILE
    nlt, tpb = N_LAT // t, SEQ // t
    grid_spec = pltpu.PrefetchScalarGridSpec(
        num_scalar_prefetch=2,
        grid=(n // t,),
        in_specs=[
            pl.BlockSpec((t, D_MODEL), lambda i, p0, p1: (i, 0)),
            pl.BlockSpec((None, None, None, 1, D_MODEL),
                         lambda i, p0, p1: (layer, _row_group(i, nlt, tpb), 5, 0, 0)),
            pl.BlockSpec((t, 1), lambda i, p0, p1: (i, 0)),
            pl.BlockSpec((t, 1), lambda i, p0, p1: (i, 0)),
            pl.BlockSpec(memory_space=pl.ANY),
        ],
        out_specs=pl.BlockSpec((t, D_MODEL), lambda i, p0, p1: (i, 0)),
        scratch_shapes=[pltpu.VMEM((t, D_MODEL), F32), pltpu.VMEM((t, D_MODEL), F32),
                        pltpu.SemaphoreType.DMA(())],
    )
    return pl.pallas_call(
        _combine_kernel,
        grid_spec=grid_spec,
        out_shape=jax.ShapeDtypeStruct((n, D_MODEL), F32),
        compiler_params=_cparams(("arbitrary",)),
        name="moe_combine",
    )(pos0, pos1, x, mod5, w0, w1, ys)


def _moe_layer(x, norm_g4, mod5, router_wt, router_b, w_gate, w_up, w_down, layer):
    hp, logits_t = _norm2(x, norm_g4, mod5, router_wt, layer)
    idx, wts, counts = _router(logits_t, router_b)
    base, item_src, expert, nblk = _plan_items(counts[:, 0])
    pos0 = base[idx[0]] + idx[2]
    pos1 = base[idx[1]] + idx[3]
    hs = _dispatch(hp, pos0, pos1)
    a = _gate_up(hs, w_gate, w_up, layer, item_src, expert, nblk)
    ys = _down(a, w_down, layer, item_src, expert, nblk)
    return _combine(x, ys, pos0, pos1, wts[0][:, None], wts[1][:, None], mod5, layer)


def _final_norm_kernel(x_ref, g_ref, o_ref):
    x = x_ref[...]
    o_ref[...] = x * lax.rsqrt(jnp.mean(x * x, axis=-1, keepdims=True) + NORM_EPS) * g_ref[...]


def _final_norm(x, g):
    n = x.shape[0]
    return pl.pallas_call(
        _final_norm_kernel,
        grid=(n // TM,),
        in_specs=[pl.BlockSpec((TM, D_MODEL), lambda i: (i, 0)), pl.BlockSpec((1, D_MODEL), lambda i: (0, 0))],
        out_specs=pl.BlockSpec((TM, D_MODEL), lambda i: (i, 0)),
        out_shape=jax.ShapeDtypeStruct((n, D_MODEL), F32),
        compiler_params=_cparams(("arbitrary",)),
        name="final_norm",
    )(x, g.reshape(1, D_MODEL))


def _attn_mixer(h1, x, mod5, layer, j, attn_w_qkv, attn_sink, attn_w_o, rope_cos, rope_sin, need_ctx):
    q_cols = A_HQ * A_DH
    epi = functools.partial(_epi_rope, rope_cols=(0, q_cols + A_HKV * A_DH), scale_cols=q_cols, scale=A_SCALE)
    extras = [(rope_cos, _rope_spec(TM_PROJ)), (rope_sin, _rope_spec(TM_PROJ))]
    qkv = _proj(h1, 0, attn_w_qkv, j, q_cols + 2 * A_HKV * A_DH, 512, epi, extras, BF16, "attn_qkv_proj")
    o = _gqa_attention(qkv, attn_sink[j].astype(F32), need_ctx)
    return _out_proj_resid(o, attn_w_o, j, x, mod5, layer, "attn_out_proj")


def _fnet_mixer(h1, x, mod5, layer, j, fnet_w_o, need_ctx):
    cc, sc = _dft_mats(F_GROUP_DIM, F_GROUP_DIM ** -0.5)
    w1 = jnp.concatenate([cc, sc], axis=1).astype(BF16)
    outs = []
    for seq_len, row_blk0, active in ((SEQ, 0, True), (CTX_LEN, N_LAT // min(TM, CTX_LEN), need_ctx)):
        if not active:
            continue
        cl, sl = _dft_mats(seq_len, seq_len ** -0.5)
        a2 = jnp.concatenate([cl, -sl], axis=1).astype(BF16)
        z = _fnet_stage1(h1, w1, row_blk0, seq_len, BATCH)
        y = _fnet_stage2(a2, z, seq_len, BATCH)
        outs.append(y.reshape(BATCH * seq_len, D_MODEL))
    y_all = jnp.concatenate(outs, axis=0) if len(outs) > 1 else outs[0]
    return _out_proj_resid(y_all, fnet_w_o, j, x, mod5, layer, "fnet_out_proj")


def _mla_mixer(h1, x, mod5, layer, j, mla_w_in, mla_g_q, mla_g_kv, mla_w_uq, mla_w_ukv, mla_w_o,
               rope_cos, rope_sin, need_ctx):
    g_cat = jnp.concatenate([mla_g_q[j], mla_g_kv[j]]).reshape(1, M_QR + M_KVR)
    extras = [(g_cat, pl.BlockSpec((1, M_QR), lambda jj, i, k: (0, jj)))]
    cqkv = _proj(h1, 0, mla_w_in, j, M_QR + M_KVR, M_QR, _epi_rmsnorm, extras, BF16, "mla_in_proj")
    w_kr = jnp.pad(mla_w_in[j][:, M_QR + M_KVR:], ((0, 0), (0, LANES - M_ROPE)))
    epi_kr = functools.partial(_epi_rope, rope_cols=(0, LANES), scale_cols=0, scale=1.0)
    rope_extras = [(rope_cos, _rope_spec(TM_PROJ)), (rope_sin, _rope_spec(TM_PROJ))]
    kr = _proj(h1, 0, w_kr, 0, LANES, LANES, epi_kr, rope_extras, BF16, "mla_kr_proj")
    w_uq = mla_w_uq[j].reshape(M_QR, M_H, M_NOPE + M_ROPE)
    w_qn = w_uq[:, :, :M_NOPE].reshape(M_QR, M_H * M_NOPE)
    w_qr = jnp.pad(w_uq[:, :, M_NOPE:], ((0, 0), (0, 0), (0, LANES - M_ROPE))).reshape(M_QR, M_H * LANES)
    w_q = jnp.concatenate([w_qn, w_qr], axis=1)
    nope_cols = M_H * M_NOPE
    epi_q = functools.partial(_epi_rope, rope_cols=(nope_cols, nope_cols + M_H * LANES),
                              scale_cols=nope_cols + M_H * LANES, scale=M_SCALE)
    qcat = _proj(cqkv, 0, w_q, 0, nope_cols + M_H * LANES, 1024, epi_q, rope_extras, BF16, "mla_q_proj")
    kv = _proj(cqkv, 1, mla_w_ukv, j, M_H * (M_NOPE + M_V), 1024, _epi_plain, [], BF16, "mla_kv_proj")
    o = _mla_attention(qcat, kv, kr, latent=True)
    if need_ctx:
        o = jnp.concatenate([o, _mla_attention(qcat, kv, kr, latent=False)], axis=0)
    return _out_proj_resid(o, mla_w_o, j, x, mod5, layer, "mla_out_proj")


def kernel(x, c, ctx, c_ctx, ada_w, ada_b, norm_g, final_g, attn_w_qkv, attn_sink, attn_w_o,
           fnet_w_o, mla_w_in, mla_g_q, mla_g_kv, mla_w_uq, mla_w_ukv, mla_w_o,
           router_w, router_b, moe_w_gate, moe_w_up, moe_w_down):
    xt = jnp.concatenate([x.reshape(N_LAT, D_MODEL), ctx.reshape(N_CTX, D_MODEL)], axis=0)
    src8 = jnp.concatenate([c, c_ctx[None], jnp.zeros((8 - BATCH - 1, D_MODEL), F32)], axis=0)
    mod5 = _ada_all(src8, ada_w, ada_b).reshape(DEPTH, 8, 6, 1, D_MODEL)
    norm_g4 = norm_g.reshape(DEPTH, 2, 1, D_MODEL)
    router_wt = router_w.T
    rope_cos, rope_sin = _rope_tables(TM_PROJ)
    for i in range(DEPTH):
        need_ctx = i < DEPTH - 1
        kind, j = i % N_MIXERS, i // N_MIXERS
        h1 = _norm1(xt, norm_g4, mod5, i)
        if kind == 0:
            x_mix = _attn_mixer(h1, xt, mod5, i, j, attn_w_qkv, attn_sink, attn_w_o, rope_cos, rope_sin, need_ctx)
        elif kind == 1:
            x_mix = _fnet_mixer(h1, xt, mod5, i, j, fnet_w_o, need_ctx)
        else:
            x_mix = _mla_mixer(h1, xt, mod5, i, j, mla_w_in, mla_g_q, mla_g_kv, mla_w_uq, mla_w_ukv, mla_w_o,
                               rope_cos, rope_sin, need_ctx)
        xt = _moe_layer(x_mix, norm_g4, mod5, router_wt, router_b, moe_w_gate, moe_w_up, moe_w_down, i)
    return _final_norm(xt[:N_LAT], final_g).reshape(BATCH, SEQ, D_MODEL)
```

```python
import functools
import math

import jax
import jax.numpy as jnp
import numpy as np
from jax import lax
from jax.experimental import pallas as pl
from jax.experimental.pallas import tpu as pltpu

D_MODEL = 2048
BATCH = 2
SEQ = 4096
DEPTH = 4
GRID_W = 64
CTX_LEN = 256
N_MIXERS = 3

BLK = 128
A_HQ = 32
A_HKV = 4
A_GROUPS = A_HQ // A_HKV
A_DH = 64
A_SCALE = A_DH ** -0.5

F_GROUPS = 8
F_GROUP_DIM = D_MODEL // F_GROUPS

M_H = 16
M_QR = 512
M_KVR = 512
M_NOPE = 128
M_ROPE = 64
M_V = 128
M_SCALE = (M_NOPE + M_ROPE) ** -0.5

N_EXPERTS = 16
N_EXPERT_GROUPS = 4
EXPERTS_PER_GROUP = N_EXPERTS // N_EXPERT_GROUPS
D_FF = 1408

ROPE_THETA = 10000.0
NORM_EPS = 1e-6
NEG_INF = -1e30

N_LAT = BATCH * SEQ
N_CTX = BATCH * CTX_LEN
N_TOK = N_LAT + N_CTX

LANES = 128
TM = 512
TM_PROJ = 1024
ROW_BLK = 256
ITEM_ROWS = 1536
N_ITEMS = N_EXPERTS + (2 * N_TOK) // ITEM_ROWS
FF_TILE = 128
DOWN_TILE = 256
VMEM_LIMIT = 56 * 1024 * 1024

F32 = jnp.float32
BF16 = jnp.bfloat16


def _cparams(sem):
    return pltpu.CompilerParams(dimension_semantics=sem, vmem_limit_bytes=VMEM_LIMIT)


def _row_group(i, n_lat_tiles, tiles_per_batch):
    return jnp.where(i < n_lat_tiles, i // tiles_per_batch, BATCH)


def _ada_kernel(src_ref, w_ref, b_ref, o_ref):
    s = src_ref[...]
    s = (s * jax.nn.sigmoid(s)).astype(BF16)
    acc = jnp.dot(s, w_ref[...].astype(BF16), preferred_element_type=F32)
    o_ref[...] = acc + b_ref[...]


def _ada_all(src8, ada_w, ada_b):
    tn = 1024
    n6 = ada_w.shape[-1]
    return pl.pallas_call(
        _ada_kernel,
        grid=(DEPTH, n6 // tn),
        in_specs=[
            pl.BlockSpec((8, D_MODEL), lambda l, j: (0, 0)),
            pl.BlockSpec((None, D_MODEL, tn), lambda l, j: (l, 0, j)),
            pl.BlockSpec((None, 1, tn), lambda l, j: (l, 0, j)),
        ],
        out_specs=pl.BlockSpec((None, 8, tn), lambda l, j: (l, 0, j)),
        out_shape=jax.ShapeDtypeStruct((DEPTH, 8, n6), F32),
        compiler_params=_cparams(("arbitrary", "arbitrary")),
        name="ada_ln",
    )(src8, ada_w, ada_b.reshape(DEPTH, 1, n6))


def _mod_spec(layer, which, n_lat_tiles, tiles_per_batch, ngrid):
    def imap(*idx):
        return (layer, _row_group(idx[0], n_lat_tiles, tiles_per_batch), which, 0, 0)
    del ngrid
    return pl.BlockSpec((None, None, None, 1, D_MODEL), imap)


def _norm_mod(x, g, sh, sc):
    y = x * lax.rsqrt(jnp.mean(x * x, axis=-1, keepdims=True) + NORM_EPS) * g
    return y * (1.0 + sc) + sh


def _norm1_kernel(x_ref, g_ref, sh_ref, sc_ref, o_ref):
    o_ref[...] = _norm_mod(x_ref[...], g_ref[...], sh_ref[...], sc_ref[...]).astype(o_ref.dtype)


def _norm1(x, norm_g4, mod5, layer):
    n = x.shape[0]
    nlt, tpb = N_LAT // TM, SEQ // TM
    return pl.pallas_call(
        _norm1_kernel,
        grid=(n // TM,),
        in_specs=[
            pl.BlockSpec((TM, D_MODEL), lambda i: (i, 0)),
            pl.BlockSpec((None, None, 1, D_MODEL), lambda i: (layer, 0, 0, 0)),
            _mod_spec(layer, 0, nlt, tpb, 1),
            _mod_spec(layer, 1, nlt, tpb, 1),
        ],
        out_specs=pl.BlockSpec((TM, D_MODEL), lambda i: (i, 0)),
        out_shape=jax.ShapeDtypeStruct((n, D_MODEL), BF16),
        compiler_params=_cparams(("arbitrary",)),
        name="norm1",
    )(x, norm_g4, mod5, mod5)


def _norm2_kernel(x_ref, g_ref, sh_ref, sc_ref, rw_ref, hp_ref, lg_ref):
    h = _norm_mod(x_ref[...], g_ref[...], sh_ref[...], sc_ref[...])
    lg_ref[...] = lax.dot_general(rw_ref[...], h, (((1,), (1,)), ((), ())),
                                  precision=lax.Precision.HIGHEST, preferred_element_type=F32)
    half = D_MODEL // 2
    hp_ref[...] = pltpu.pack_elementwise([h[:, :half], h[:, half:]], packed_dtype=BF16)


def _norm2(x, norm_g4, mod5, router_wt, layer):
    n = x.shape[0]
    nlt, tpb = N_LAT // TM, SEQ // TM
    return pl.pallas_call(
        _norm2_kernel,
        grid=(n // TM,),
        in_specs=[
            pl.BlockSpec((TM, D_MODEL), lambda i: (i, 0)),
            pl.BlockSpec((None, None, 1, D_MODEL), lambda i: (layer, 1, 0, 0)),
            _mod_spec(layer, 3, nlt, tpb, 1),
            _mod_spec(layer, 4, nlt, tpb, 1),
            pl.BlockSpec((N_EXPERTS, D_MODEL), lambda i: (0, 0)),
        ],
        out_specs=[
            pl.BlockSpec((TM, D_MODEL // 2), lambda i: (i, 0)),
            pl.BlockSpec((N_EXPERTS, TM), lambda i: (0, i)),
        ],
        out_shape=[
            jax.ShapeDtypeStruct((n, D_MODEL // 2), jnp.uint32),
            jax.ShapeDtypeStruct((N_EXPERTS, n), F32),
        ],
        compiler_params=_cparams(("arbitrary",)),
        name="norm2_router_logits",
    )(x, norm_g4, mod5, mod5, router_wt)


def _mm_kernel(*refs, nk, n_extra, epilogue):
    a_ref, w_ref = refs[0], refs[1]
    extra = refs[2:2 + n_extra]
    o_ref = refs[2 + n_extra]
    part = jnp.dot(a_ref[...].astype(BF16), w_ref[...].astype(BF16), preferred_element_type=F32)
    if nk == 1:
        epilogue(part, o_ref, *extra)
        return
    acc_ref = refs[3 + n_extra]
    k = pl.program_id(2)

    @pl.when(k == 0)
    def _():
        acc_ref[...] = part

    @pl.when(k > 0)
    def _():
        acc_ref[...] += part

    @pl.when(k == nk - 1)
    def _():
        epilogue(acc_ref[...], o_ref, *extra)


def _matmul(a, a_spec, w, w_spec, extras, out_shape, out_spec, grid, epilogue, tm, tn, name):
    nk = grid[2]
    kern = functools.partial(_mm_kernel, nk=nk, n_extra=len(extras), epilogue=epilogue)
    scratch = [pltpu.VMEM((tm, tn), F32)] if nk > 1 else []
    return pl.pallas_call(
        kern,
        grid=grid,
        in_specs=[a_spec, w_spec] + [s for _, s in extras],
        out_specs=out_spec,
        out_shape=out_shape,
        scratch_shapes=scratch,
        compiler_params=_cparams(("arbitrary", "arbitrary", "arbitrary")),
        name=name,
    )(a, w, *[x for x, _ in extras])


def _epi_plain(acc, o_ref):
    o_ref[...] = acc.astype(o_ref.dtype)


def _epi_resid_gate(acc, o_ref, x_ref, gate_ref):
    o_ref[...] = x_ref[...] + gate_ref[...] * acc


def _epi_rmsnorm(acc, o_ref, g_ref):
    y = acc * lax.rsqrt(jnp.mean(acc * acc, axis=-1, keepdims=True) + NORM_EPS) * g_ref[...]
    o_ref[...] = y.astype(o_ref.dtype)


def _rope_rotate(acc, cos_ref, sin_ref):
    tn = acc.shape[1]
    reps = tn // LANES
    cos = jnp.concatenate([cos_ref[...]] * reps, axis=1) if reps > 1 else cos_ref[...]
    sin = jnp.concatenate([sin_ref[...]] * reps, axis=1) if reps > 1 else sin_ref[...]
    up = pltpu.roll(acc, tn - 16, 1)
    dn = pltpu.roll(acc, 16, 1)
    lane = lax.broadcasted_iota(jnp.int32, acc.shape, 1)
    rot = jnp.where((lane % 32) < 16, up, dn)
    return acc * cos + rot * sin


def _epi_rope(acc, o_ref, cos_ref, sin_ref, *, rope_cols, scale_cols, scale):
    tn = acc.shape[1]
    col = pl.program_id(0) * tn + lax.broadcasted_iota(jnp.int32, acc.shape, 1)
    r = _rope_rotate(acc, cos_ref, sin_ref)
    r = jnp.where((col >= rope_cols[0]) & (col < rope_cols[1]), r, acc)
    if scale_cols:
        r = jnp.where(col < scale_cols, r * scale, r)
    o_ref[...] = r.astype(o_ref.dtype)


def _rope_tables(tm):
    d_axis = A_DH // 2
    inv_freq = ROPE_THETA ** (-jnp.arange(0, d_axis, 2, dtype=F32) / d_axis)
    t = jnp.arange(SEQ, dtype=jnp.int32)
    ang_r = (t // GRID_W).astype(F32)[:, None] * inv_freq
    ang_c = (t % GRID_W).astype(F32)[:, None] * inv_freq
    ang = jnp.concatenate([ang_r, ang_r, ang_c, ang_c] * 2, axis=1)
    sign = jnp.tile(jnp.concatenate([-jnp.ones(16, F32), jnp.ones(16, F32)]), 4)
    cos = jnp.concatenate([jnp.cos(ang), jnp.ones((tm, LANES), F32)], axis=0)
    sin = jnp.concatenate([jnp.sin(ang) * sign, jnp.zeros((tm, LANES), F32)], axis=0)
    return cos, sin


def _rope_spec(tm):
    nlt, tpb = N_LAT // tm, SEQ // tm
    return pl.BlockSpec((tm, LANES), lambda j, i, k: (jnp.where(i < nlt, i % tpb, tpb), 0))


def _proj(a, a_col, w, layer, n_out, tn, epilogue, extras, out_dtype, name):
    n = a.shape[0]
    kdim = w.shape[-2]
    grid = (n_out // tn, pl.cdiv(n, TM_PROJ), 1)
    if w.ndim == 3:
        w_spec = pl.BlockSpec((None, kdim, tn), lambda j, i, k: (layer, 0, j))
    else:
        w_spec = pl.BlockSpec((kdim, tn), lambda j, i, k: (0, j))
    return _matmul(
        a, pl.BlockSpec((TM_PROJ, kdim), lambda j, i, k: (i, a_col)), w, w_spec, extras,
        jax.ShapeDtypeStruct((n, n_out), out_dtype), pl.BlockSpec((TM_PROJ, tn), lambda j, i, k: (i, j)),
        grid, epilogue, TM_PROJ, tn, name)


def _out_proj_resid(o, w3, layer_w, x, mod5, layer, name):
    nlt, tpb = N_LAT // TM_PROJ, SEQ // TM_PROJ
    tn = 1024
    gate_spec = pl.BlockSpec(
        (None, None, None, 1, tn),
        lambda j, i, k: (layer, _row_group(i, nlt, tpb), 2, 0, j))
    extras = [(x, pl.BlockSpec((TM_PROJ, tn), lambda j, i, k: (i, j))), (mod5, gate_spec)]
    return _proj(o, 0, w3, layer_w, D_MODEL, tn, _epi_resid_gate, extras, F32, name)


def _dup_half(t, upper):
    h = t[:, A_DH:] if upper else t[:, :A_DH]
    return jnp.concatenate([h, h], axis=1)


def _gqa_kernel(sink_ref, q_ref, kp_ref, ks_ref, kn_ref, vp_ref, vs_ref, vn_ref, kc_ref, vc_ref, o_ref,
                *, n_lat_blocks, blocks_per_seq):
    n = pl.program_id(0)
    is_lat = n < n_lat_blocks
    nb = n % blocks_per_seq
    iq = lax.broadcasted_iota(jnp.int32, (BLK, 3 * BLK), 0)
    ik = lax.broadcasted_iota(jnp.int32, (BLK, 3 * BLK), 1)
    mask = (ik >= iq) & (ik <= iq + 2 * BLK)
    mask = mask & ((nb > 0) | (ik >= BLK)) & ((nb < blocks_per_seq - 1) | (ik < 2 * BLK)) & is_lat
    lane = lax.broadcasted_iota(jnp.int32, (BLK, LANES), 1)
    lo_half = lane < A_DH
    keep_lo = lo_half.astype(F32).astype(BF16)
    keep_hi = (1.0 - lo_half.astype(F32)).astype(BF16)
    n_pairs = A_GROUPS // 2
    for hk in range(A_HKV):
        c0 = (hk // 2) * LANES
        upper = (hk % 2) == 1
        k_all = jnp.concatenate([
            _dup_half(kp_ref[:, c0:c0 + LANES], upper), _dup_half(ks_ref[:, c0:c0 + LANES], upper),
            _dup_half(kn_ref[:, c0:c0 + LANES], upper), _dup_half(kc_ref[:, c0:c0 + LANES], upper)], axis=0)
        v_all = jnp.concatenate([
            _dup_half(vp_ref[:, c0:c0 + LANES], upper), _dup_half(vs_ref[:, c0:c0 + LANES], upper),
            _dup_half(vn_ref[:, c0:c0 + LANES], upper), _dup_half(vc_ref[:, c0:c0 + LANES], upper)], axis=0)
        lhs = []
        for p in range(n_pairs):
            qp = q_ref[:, hk * A_GROUPS * A_DH + p * LANES: hk * A_GROUPS * A_DH + (p + 1) * LANES]
            lhs.append(qp * keep_lo)
            lhs.append(qp * keep_hi)
        lhs = jnp.concatenate(lhs, axis=0)
        s = lax.dot_general(lhs, k_all, (((1,), (1,)), ((), ())), preferred_element_type=F32)
        probs, denoms = [], []
        for g in range(A_GROUPS):
            sg = s[g * BLK:(g + 1) * BLK]
            s_loc = jnp.where(mask, sg[:, :3 * BLK], NEG_INF)
            s_ctx = sg[:, 3 * BLK:]
            sink = sink_ref[hk * A_GROUPS + g]
            m = jnp.maximum(jnp.maximum(jnp.max(s_loc, axis=-1, keepdims=True),
                                        jnp.max(s_ctx, axis=-1, keepdims=True)), sink)
            p_loc = jnp.exp(s_loc - m)
            p_ctx = jnp.exp(s_ctx - m)
            denoms.append(jnp.sum(p_loc, axis=-1, keepdims=True) + jnp.sum(p_ctx, axis=-1, keepdims=True)
                          + jnp.exp(sink - m))
            probs.append(jnp.concatenate([p_loc, p_ctx], axis=1).astype(BF16))
        o_all = jnp.dot(jnp.concatenate(probs, axis=0), v_all, preferred_element_type=F32)
        outs = [o_all[g * BLK:(g + 1) * BLK] / denoms[g] for g in range(A_GROUPS)]
        for p in range(n_pairs):
            col = hk * A_GROUPS * A_DH + p * LANES
            o_ref[:, col:col + LANES] = jnp.where(lo_half, outs[2 * p], outs[2 * p + 1]).astype(o_ref.dtype)


def _gqa_attention(qkv, sink, need_ctx):
    bps = SEQ // BLK
    nlb = BATCH * bps
    n_ctx_blocks = N_CTX // BLK if need_ctx else 0
    nq = nlb + n_ctx_blocks
    kcol, vcol = (A_HQ * A_DH) // 256, (A_HQ * A_DH) // 256 + 1
    cpb = CTX_LEN // BLK

    def batch_of(n):
        return jnp.where(n < nlb, n // bps, (n - nlb) // cpb)

    def loc_map(delta, col):
        def imap(n, s):
            nb = n % bps
            blk = jnp.clip(nb + delta, 0, bps - 1)
            return (jnp.where(n < nlb, (n // bps) * bps + blk, 0), col)
        return imap

    def ctx_map(col):
        return lambda n, s: (N_LAT // CTX_LEN + batch_of(n), col)

    kv_blk = (BLK, 256)
    grid_spec = pltpu.PrefetchScalarGridSpec(
        num_scalar_prefetch=1,
        grid=(nq,),
        in_specs=[
            pl.BlockSpec((BLK, A_HQ * A_DH), lambda n, s: (n, 0)),
            pl.BlockSpec(kv_blk, loc_map(-1, kcol)), pl.BlockSpec(kv_blk, loc_map(0, kcol)),
            pl.BlockSpec(kv_blk, loc_map(1, kcol)),
            pl.BlockSpec(kv_blk, loc_map(-1, vcol)), pl.BlockSpec(kv_blk, loc_map(0, vcol)),
            pl.BlockSpec(kv_blk, loc_map(1, vcol)),
            pl.BlockSpec((CTX_LEN, 256), ctx_map(kcol)), pl.BlockSpec((CTX_LEN, 256), ctx_map(vcol)),
        ],
        out_specs=pl.BlockSpec((BLK, A_HQ * A_DH), lambda n, s: (n, 0)),
    )
    kern = functools.partial(_gqa_kernel, n_lat_blocks=nlb, blocks_per_seq=bps)
    return pl.pallas_call(
        kern,
        grid_spec=grid_spec,
        out_shape=jax.ShapeDtypeStruct((nq * BLK, A_HQ * A_DH), BF16),
        compiler_params=_cparams(("arbitrary",)),
        name="gqa_window_attn",
    )(sink, qkv, qkv, qkv, qkv, qkv, qkv, qkv, qkv, qkv)


MLA_TQ = 256
MLA_TK = 512


def _mla_kernel(*refs, n_lat_keys):
    if n_lat_keys:
        (qn_ref, qr_ref, kn_ref, v_ref, kr_ref, knc_ref, vc_ref, krc_ref, o_ref, kcat, vcat) = refs
    else:
        (qn_ref, qr_ref, knc_ref, vc_ref, krc_ref, o_ref, kcat, vcat) = refs
    qi = pl.program_id(2)
    nk = n_lat_keys + CTX_LEN

    @pl.when(qi == 0)
    def _():
        if n_lat_keys:
            kcat[0:n_lat_keys, 0:LANES] = kn_ref[...]
            kcat[0:n_lat_keys, LANES:2 * LANES] = kr_ref[...]
            vcat[0:n_lat_keys, 0:M_V] = v_ref[...]
        kcat[n_lat_keys:nk, 0:LANES] = knc_ref[...]
        kcat[n_lat_keys:nk, LANES:2 * LANES] = krc_ref[...]
        vcat[n_lat_keys:nk, 0:M_V] = vc_ref[...]
        vcat[:, M_V:M_V + LANES] = jnp.ones((nk, LANES), BF16)

    q = jnp.concatenate([qn_ref[...], qr_ref[...]], axis=1)

    def chunk(k0, size):
        kc = kcat[k0:k0 + size, :]
        vc = vcat[k0:k0 + size, :]
        s = lax.dot_general(q, kc, (((1,), (1,)), ((), ())), preferred_element_type=F32)
        m_c = jnp.max(s, axis=-1, keepdims=True)
        p = jnp.exp((s - m_c).astype(BF16))
        ol = jnp.dot(p, vc, preferred_element_type=F32)
        return m_c, ol[:, M_V:M_V + 1], ol[:, :M_V]

    parts = [chunk(c * MLA_TK, MLA_TK) for c in range(n_lat_keys // MLA_TK)]
    parts.append(chunk(n_lat_keys, CTX_LEN))
    m = parts[0][0]
    for m_c, _, _ in parts[1:]:
        m = jnp.maximum(m, m_c)
    l = None
    acc = None
    for m_c, l_c, o_c in parts:
        w_c = jnp.exp(m_c - m)
        l = w_c * l_c if l is None else l + w_c * l_c
        acc = w_c * o_c if acc is None else acc + w_c * o_c
    o_ref[...] = (acc / l).astype(o_ref.dtype)


def _mla_attention(qcat, kv, kr, latent):
    hq = M_H
    ctx_row_blk = N_LAT // CTX_LEN
    if latent:
        tq, nqt, n_lat_keys = MLA_TQ, SEQ // MLA_TQ, SEQ
        row0 = lambda b, qi: b * nqt + qi
        n_out = N_LAT
    else:
        tq, nqt, n_lat_keys = CTX_LEN, 1, 0
        row0 = lambda b, qi: ctx_row_blk + b
        n_out = N_CTX
    in_specs = [
        pl.BlockSpec((tq, LANES), lambda b, h, qi: (row0(b, qi), h)),
        pl.BlockSpec((tq, LANES), lambda b, h, qi: (row0(b, qi), hq + h)),
    ]
    args = [qcat, qcat]
    if latent:
        in_specs += [
            pl.BlockSpec((SEQ, LANES), lambda b, h, qi: (b, 2 * h)),
            pl.BlockSpec((SEQ, LANES), lambda b, h, qi: (b, 2 * h + 1)),
            pl.BlockSpec((SEQ, LANES), lambda b, h, qi: (b, 0)),
        ]
        args += [kv, kv, kr]
    in_specs += [
        pl.BlockSpec((CTX_LEN, LANES), lambda b, h, qi: (ctx_row_blk + b, 2 * h)),
        pl.BlockSpec((CTX_LEN, LANES), lambda b, h, qi: (ctx_row_blk + b, 2 * h + 1)),
        pl.BlockSpec((CTX_LEN, LANES), lambda b, h, qi: (ctx_row_blk + b, 0)),
    ]
    args += [kv, kv, kr]
    if latent:
        out_spec = pl.BlockSpec((tq, LANES), lambda b, h, qi: (b * nqt + qi, h))
    else:
        out_spec = pl.BlockSpec((tq, LANES), lambda b, h, qi: (b, h))
    nk = n_lat_keys + CTX_LEN
    return pl.pallas_call(
        functools.partial(_mla_kernel, n_lat_keys=n_lat_keys),
        grid=(BATCH, hq, nqt),
        in_specs=in_specs,
        out_specs=out_spec,
        out_shape=jax.ShapeDtypeStruct((n_out, hq * M_V), BF16),
        scratch_shapes=[pltpu.VMEM((nk, 2 * LANES), BF16), pltpu.VMEM((nk, M_V + LANES), BF16)],
        compiler_params=_cparams(("arbitrary", "arbitrary", "arbitrary")),
        name="mla_attn_lat" if latent else "mla_attn_ctx",
    )(*args)


def _dft_mats(n, scale):
    sub = 64
    k = jnp.arange(n, dtype=jnp.int32)
    j1 = jnp.arange(n // sub, dtype=jnp.int32) * sub
    j0 = jnp.arange(sub, dtype=jnp.int32)
    a = ((j1[:, None] * k[None, :]) % n).astype(F32) * (2.0 * math.pi / n)
    b = ((j0[:, None] * k[None, :]) % n).astype(F32) * (2.0 * math.pi / n)
    ca, sa = (jnp.cos(a) * scale)[:, None, :], (jnp.sin(a) * scale)[:, None, :]
    cb, sb = jnp.cos(b)[None, :, :], jnp.sin(b)[None, :, :]
    return (ca * cb - sa * sb).reshape(n, n), (sa * cb + ca * sb).reshape(n, n)


def _fnet_stage1_kernel(h_ref, w_ref, z_ref):
    z = jnp.dot(h_ref[...], w_ref[...], preferred_element_type=F32)
    z_ref[0] = z[:, :F_GROUP_DIM].astype(z_ref.dtype)
    z_ref[1] = z[:, F_GROUP_DIM:].astype(z_ref.dtype)


def _fnet_stage1(h1, w1, row_blk0, seq_len, nbatch):
    tm = min(TM, seq_len)
    tiles = seq_len // tm
    return pl.pallas_call(
        _fnet_stage1_kernel,
        grid=(nbatch, tiles, F_GROUPS),
        in_specs=[
            pl.BlockSpec((tm, F_GROUP_DIM), lambda b, i, g: (row_blk0 + b * tiles + i, g)),
            pl.BlockSpec((F_GROUP_DIM, 2 * F_GROUP_DIM), lambda b, i, g: (0, 0)),
        ],
        out_specs=pl.BlockSpec((None, 2, tm, F_GROUP_DIM), lambda b, i, g: (b, 0, i, g)),
        out_shape=jax.ShapeDtypeStruct((nbatch, 2, seq_len, D_MODEL), BF16),
        compiler_params=_cparams(("arbitrary", "arbitrary", "arbitrary")),
        name="fnet_channel_dft",
    )(h1, w1)


def _fnet_stage2_kernel(a_ref, z_ref, o_ref, acc_ref, *, nk):
    k = pl.program_id(2)
    part = jnp.dot(a_ref[...], z_ref[...], preferred_element_type=F32)

    @pl.when(k == 0)
    def _():
        acc_ref[...] = part

    @pl.when(k > 0)
    def _():
        acc_ref[...] += part

    @pl.when(k == nk - 1)
    def _():
        o_ref[...] = acc_ref[...].astype(o_ref.dtype)


def _fnet_stage2(a2, z, seq_len, nbatch):
    tm = min(1024, seq_len)
    tk = min(512, seq_len)
    nk = 2 * seq_len // tk
    z2 = z.reshape(nbatch, 2 * seq_len, D_MODEL)
    return pl.pallas_call(
        functools.partial(_fnet_stage2_kernel, nk=nk),
        grid=(nbatch, seq_len // tm, nk),
        in_specs=[
            pl.BlockSpec((tm, tk), lambda b, i, k: (i, k)),
            pl.BlockSpec((None, tk, D_MODEL), lambda b, i, k: (b, k, 0)),
        ],
        out_specs=pl.BlockSpec((None, tm, D_MODEL), lambda b, i, k: (b, i, 0)),
        out_shape=jax.ShapeDtypeStruct((nbatch, seq_len, D_MODEL), BF16),
        scratch_shapes=[pltpu.VMEM((tm, D_MODEL), F32)],
        compiler_params=_cparams(("arbitrary", "arbitrary", "arbitrary")),
        name="fnet_position_dft",
    )(a2, z2)


ROUTER_TILE = 512


def _top2_rows(v, rowf):
    big = float(v.shape[0])
    m1 = jnp.max(v, axis=0, keepdims=True)
    i1 = jnp.min(jnp.where(v == m1, rowf, big), axis=0, keepdims=True)
    v2 = jnp.where(rowf == i1, NEG_INF, v)
    m2 = jnp.max(v2, axis=0, keepdims=True)
    i2 = jnp.min(jnp.where(v2 == m2, rowf, big), axis=0, keepdims=True)
    return m1, i1, m2, i2


def _router_kernel(lg_ref, b_ref, idx_ref, w_ref, cnt_ref, carry_ref):
    step = pl.program_id(0)
    t = lg_ref.shape[1]

    @pl.when(step == 0)
    def _():
        carry_ref[...] = jnp.zeros_like(carry_ref)

    scores = jax.nn.sigmoid(lg_ref[...])
    sel = scores + b_ref[...][:, 0:1]
    row = lax.broadcasted_iota(jnp.int32, (N_EXPERTS, t), 0)
    rowf = row.astype(F32)
    grp = row // EXPERTS_PER_GROUP
    best = jnp.zeros((1, t), jnp.int32)
    best_v = None
    for g in range(N_EXPERT_GROUPS):
        m1, _, m2, _ = _top2_rows(jnp.where(grp == g, sel, NEG_INF), rowf)
        gs = m1 + m2
        if g == 0:
            best_v = gs
        else:
            upd = gs > best_v
            best = jnp.where(upd, g, best)
            best_v = jnp.where(upd, gs, best_v)
    _, e0, _, e1 = _top2_rows(jnp.where(grp == best, sel, NEG_INF), rowf)
    oh0 = rowf == e0
    oh1 = rowf == e1
    s0 = jnp.sum(jnp.where(oh0, scores, 0.0), axis=0, keepdims=True)
    s1 = jnp.sum(jnp.where(oh1, scores, 0.0), axis=0, keepdims=True)
    tot = s0 + s1
    cnt = jnp.where(oh0 | oh1, 1.0, 0.0)
    a = lax.broadcasted_iota(jnp.int32, (t, t), 0)
    bcol = lax.broadcasted_iota(jnp.int32, (t, t), 1)
    tri = jnp.where(a < bcol, 1.0, 0.0).astype(BF16)
    excl = jnp.dot(cnt.astype(BF16), tri, preferred_element_type=F32) + carry_ref[...][:, 0:1]
    r0 = jnp.sum(jnp.where(oh0, excl, 0.0), axis=0, keepdims=True)
    r1 = jnp.sum(jnp.where(oh1, excl, 0.0), axis=0, keepdims=True)
    new_carry = carry_ref[...] + jnp.sum(cnt, axis=1, keepdims=True)
    carry_ref[...] = new_carry
    idx_ref[...] = jnp.zeros(idx_ref.shape, jnp.int32)
    idx_ref[0:1, :] = e0.astype(jnp.int32)
    idx_ref[1:2, :] = e1.astype(jnp.int32)
    idx_ref[2:3, :] = r0.astype(jnp.int32)
    idx_ref[3:4, :] = r1.astype(jnp.int32)
    w_ref[...] = jnp.zeros(w_ref.shape, F32)
    w_ref[0:1, :] = s0 / tot
    w_ref[1:2, :] = s1 / tot
    cnt_ref[...] = new_carry.astype(jnp.int32)


def _router(logits_t, router_b):
    n = logits_t.shape[1]
    t = ROUTER_TILE
    b2 = jnp.broadcast_to(router_b.astype(F32)[:, None], (N_EXPERTS, LANES))
    return pl.pallas_call(
        _router_kernel,
        grid=(n // t,),
        in_specs=[pl.BlockSpec((N_EXPERTS, t), lambda i: (0, i)),
                  pl.BlockSpec((N_EXPERTS, LANES), lambda i: (0, 0))],
        out_specs=[pl.BlockSpec((8, t), lambda i: (0, i)),
                   pl.BlockSpec((8, t), lambda i: (0, i)),
                   pl.BlockSpec((N_EXPERTS, LANES), lambda i: (0, 0))],
        out_shape=[jax.ShapeDtypeStruct((8, n), jnp.int32),
                   jax.ShapeDtypeStruct((8, n), F32),
                   jax.ShapeDtypeStruct((N_EXPERTS, LANES), jnp.int32)],
        scratch_shapes=[pltpu.VMEM((N_EXPERTS, LANES), F32)],
        compiler_params=_cparams(("arbitrary",)),
        name="moe_router",
    )(logits_t, b2)


def _plan_items(counts):
    n_chunks = (counts + ITEM_ROWS - 1) // ITEM_ROWS
    ends = jnp.cumsum(n_chunks)
    starts = ends - n_chunks
    base = starts * ITEM_ROWS
    n_valid = ends[-1]
    item = jnp.arange(N_ITEMS, dtype=jnp.int32)
    item_src = jnp.minimum(item, n_valid - 1)
    expert = jnp.sum((ends[None, :] <= item_src[:, None]).astype(jnp.int32), axis=1)
    rows = jnp.clip(counts[expert] - (item_src - starts[expert]) * ITEM_ROWS, 0, ITEM_ROWS)
    rows = jnp.where(item < n_valid, rows, 0).astype(jnp.int32)
    nblk = (rows + ROW_BLK - 1) // ROW_BLK
    return base.astype(jnp.int32), item_src.astype(jnp.int32), expert, nblk, rows


DISPATCH_TILE = 256


def _dispatch_kernel(p0_ref, p1_ref, hp_ref, hs_ref, sem):
    base = pl.program_id(0) * DISPATCH_TILE

    def copies(r):
        t = base + r
        return (pltpu.make_async_copy(hp_ref.at[pl.ds(r, 1)], hs_ref.at[pl.ds(p0_ref[t], 1)], sem),
                pltpu.make_async_copy(hp_ref.at[pl.ds(r, 1)], hs_ref.at[pl.ds(p1_ref[t], 1)], sem))

    def start(r, c):
        a, b = copies(r)
        a.start()
        b.start()
        return c

    def wait(r, c):
        a, b = copies(r)
        a.wait()
        b.wait()
        return c

    lax.fori_loop(0, DISPATCH_TILE, start, 0, unroll=8)
    lax.fori_loop(0, DISPATCH_TILE, wait, 0, unroll=8)


def _dispatch(hp, pos0, pos1):
    n = hp.shape[0]
    grid_spec = pltpu.PrefetchScalarGridSpec(
        num_scalar_prefetch=2,
        grid=(n // DISPATCH_TILE,),
        in_specs=[pl.BlockSpec((DISPATCH_TILE, D_MODEL // 2), lambda i, p0, p1: (i, 0))],
        out_specs=pl.BlockSpec(memory_space=pl.ANY),
        scratch_shapes=[pltpu.SemaphoreType.DMA(())],
    )
    return pl.pallas_call(
        _dispatch_kernel,
        grid_spec=grid_spec,
        out_shape=jax.ShapeDtypeStruct((N_ITEMS * ITEM_ROWS, D_MODEL // 2), hp.dtype),
        compiler_params=pltpu.CompilerParams(dimension_semantics=("arbitrary",), has_side_effects=True),
        name="moe_dispatch",
    )(pos0, pos1, hp)


def _gate_up_kernel(src_ref, exp_ref, nblk_ref, rows_ref, hs_ref, wg_ref, wu_ref, a_ref, hb_ref):
    del src_ref, exp_ref
    i = pl.program_id(0)
    f = pl.program_id(1)
    nblk = nblk_ref[i]
    n_rows = rows_ref[i]
    half = D_MODEL // 2

    @pl.when(nblk > 0)
    def _():
        @pl.when(f == 0)
        def _():
            def unpack(r, c):
                rows = pl.ds(pl.multiple_of(r * ROW_BLK, ROW_BLK), ROW_BLK)
                u = hs_ref[rows, :]
                live = (r * ROW_BLK + lax.broadcasted_iota(jnp.int32, (ROW_BLK, half), 0)) < n_rows
                lo = pltpu.unpack_elementwise(u, index=0, packed_dtype=BF16, unpacked_dtype=F32)
                hi = pltpu.unpack_elementwise(u, index=1, packed_dtype=BF16, unpacked_dtype=F32)
                hb_ref[rows, 0:half] = jnp.where(live, lo, 0.0).astype(BF16)
                hb_ref[rows, half:D_MODEL] = jnp.where(live, hi, 0.0).astype(BF16)
                return c
            lax.fori_loop(0, nblk, unpack, 0)

        for m in range(1, ITEM_ROWS // ROW_BLK + 1):
            rows = m * ROW_BLK

            @pl.when(nblk == m)
            def _():
                w = jnp.concatenate([wg_ref[...].astype(BF16), wu_ref[...].astype(BF16)], axis=1)
                res = jnp.dot(hb_ref[0:rows, :], w, preferred_element_type=F32)
                g = res[:, :FF_TILE]
                u = res[:, FF_TILE:]
                a_ref[0:rows, :] = (g * jax.nn.sigmoid(g) * u).astype(a_ref.dtype)
                if rows < ITEM_ROWS:
                    a_ref[rows:ITEM_ROWS, :] = jnp.zeros((ITEM_ROWS - rows, FF_TILE), a_ref.dtype)


def _gate_up(hs, w_gate, w_up, layer, item_src, expert, nblk, rows):
    nf = D_FF // FF_TILE

    def frozen_f(i, f, nb):
        return jnp.where(nb[i] > 0, f, nf - 1)

    grid_spec = pltpu.PrefetchScalarGridSpec(
        num_scalar_prefetch=4,
        grid=(N_ITEMS, nf),
        in_specs=[
            pl.BlockSpec((ITEM_ROWS, D_MODEL // 2), lambda i, f, src, ex, nb, rw: (src[i], 0)),
            pl.BlockSpec((None, None, D_MODEL, FF_TILE),
                         lambda i, f, src, ex, nb, rw: (layer, ex[i], 0, frozen_f(i, f, nb))),
            pl.BlockSpec((None, None, D_MODEL, FF_TILE),
                         lambda i, f, src, ex, nb, rw: (layer, ex[i], 0, frozen_f(i, f, nb))),
        ],
        out_specs=pl.BlockSpec((ITEM_ROWS, FF_TILE),
                               lambda i, f, src, ex, nb, rw: (src[i], frozen_f(i, f, nb))),
        scratch_shapes=[pltpu.VMEM((ITEM_ROWS, D_MODEL), BF16)],
    )
    return pl.pallas_call(
        _gate_up_kernel,
        grid_spec=grid_spec,
        out_shape=jax.ShapeDtypeStruct((N_ITEMS * ITEM_ROWS, D_FF), BF16),
        compiler_params=_cparams(("arbitrary", "arbitrary")),
        name="moe_gate_up",
    )(item_src, expert, nblk, rows, hs, w_gate, w_up)


def _down_kernel(src_ref, exp_ref, nblk_ref, a_ref, wd_ref, y_ref):
    del src_ref, exp_ref
    i = pl.program_id(0)
    nblk = nblk_ref[i]

    for m in range(1, ITEM_ROWS // ROW_BLK + 1):
        rows = m * ROW_BLK

        @pl.when(nblk == m)
        def _():
            y_ref[0:rows, :] = jnp.dot(a_ref[0:rows, :], wd_ref[...].astype(BF16), preferred_element_type=F32)
            if rows < ITEM_ROWS:
                y_ref[rows:ITEM_ROWS, :] = jnp.zeros((ITEM_ROWS - rows, DOWN_TILE), y_ref.dtype)


def _down(a, w_down, layer, item_src, expert, nblk):
    nj = D_MODEL // DOWN_TILE

    def frozen_j(i, j, nb):
        return jnp.where(nb[i] > 0, j, nj - 1)

    grid_spec = pltpu.PrefetchScalarGridSpec(
        num_scalar_prefetch=3,
        grid=(N_ITEMS, nj),
        in_specs=[
            pl.BlockSpec((ITEM_ROWS, D_FF), lambda i, j, src, ex, nb: (src[i], 0)),
            pl.BlockSpec((None, None, D_FF, DOWN_TILE),
                         lambda i, j, src, ex, nb: (layer, ex[i], 0, frozen_j(i, j, nb))),
        ],
        out_specs=pl.BlockSpec((ITEM_ROWS, DOWN_TILE), lambda i, j, src, ex, nb: (src[i], frozen_j(i, j, nb))),
    )
    return pl.pallas_call(
        _down_kernel,
        grid_spec=grid_spec,
        out_shape=jax.ShapeDtypeStruct((N_ITEMS * ITEM_ROWS, D_MODEL), F32),
        compiler_params=_cparams(("arbitrary", "arbitrary")),
        name="moe_down",
    )(item_src, expert, nblk, a, w_down)


COMBINE_TILE = 256


def _combine_kernel(p0_ref, p1_ref, x_ref, gate_ref, w0_ref, w1_ref, ys_ref, o_ref, y0_ref, y1_ref, sem):
    base = pl.program_id(0) * COMBINE_TILE

    def copies(r):
        t = base + r
        return (pltpu.make_async_copy(ys_ref.at[pl.ds(p0_ref[t], 1)], y0_ref.at[pl.ds(r, 1)], sem),
                pltpu.make_async_copy(ys_ref.at[pl.ds(p1_ref[t], 1)], y1_ref.at[pl.ds(r, 1)], sem))

    def start(r, c):
        a, b = copies(r)
        a.start()
        b.start()
        return c

    def wait(r, c):
        a, b = copies(r)
        a.wait()
        b.wait()
        return c

    lax.fori_loop(0, COMBINE_TILE, start, 0, unroll=8)
    lax.fori_loop(0, COMBINE_TILE, wait, 0, unroll=8)
    y = w0_ref[...] * y0_ref[...] + w1_ref[...] * y1_ref[...]
    o_ref[...] = x_ref[...] + gate_ref[...] * y


def _combine(x, ys, pos0, pos1, w0, w1, mod5, layer):
    n = x.shape[0]
    t = COMBINE_TILE
    nlt, tpb = N_LAT // t, SEQ // t
    grid_spec = pltpu.PrefetchScalarGridSpec(
        num_scalar_prefetch=2,
        grid=(n // t,),
        in_specs=[
            pl.BlockSpec((t, D_MODEL), lambda i, p0, p1: (i, 0)),
            pl.BlockSpec((None, None, None, 1, D_MODEL),
                         lambda i, p0, p1: (layer, _row_group(i, nlt, tpb), 5, 0, 0)),
            pl.BlockSpec((t, 1), lambda i, p0, p1: (i, 0)),
            pl.BlockSpec((t, 1), lambda i, p0, p1: (i, 0)),
            pl.BlockSpec(memory_space=pl.ANY),
        ],
        out_specs=pl.BlockSpec((t, D_MODEL), lambda i, p0, p1: (i, 0)),
        scratch_shapes=[pltpu.VMEM((t, D_MODEL), F32), pltpu.VMEM((t, D_MODEL), F32),
                        pltpu.SemaphoreType.DMA(())],
    )
    return pl.pallas_call(
        _combine_kernel,
        grid_spec=grid_spec,
        out_shape=jax.ShapeDtypeStruct((n, D_MODEL), F32),
        compiler_params=_cparams(("arbitrary",)),
        name="moe_combine",
    )(pos0, pos1, x, mod5, w0, w1, ys)


def _moe_layer(x, norm_g4, mod5, router_wt, router_b, w_gate, w_up, w_down, layer):
    hp, logits_t = _norm2(x, norm_g4, mod5, router_wt, layer)
    idx, wts, counts = _router(logits_t, router_b)
    base, item_src, expert, nblk, rows = _plan_items(counts[:, 0])
    pos0 = base[idx[0]] + idx[2]
    pos1 = base[idx[1]] + idx[3]
    hs = _dispatch(hp, pos0, pos1)
    a = _gate_up(hs, w_gate, w_up, layer, item_src, expert, nblk, rows)
    ys = _down(a, w_down, layer, item_src, expert, nblk)
    return _combine(x, ys, pos0, pos1, wts[0][:, None], wts[1][:, None], mod5, layer)


def _final_norm_kernel(x_ref, g_ref, o_ref):
    x = x_ref[...]
    o_ref[...] = x * lax.rsqrt(jnp.mean(x * x, axis=-1, keepdims=True) + NORM_EPS) * g_ref[...]


def _final_norm(x, g):
    n = x.shape[0]
    return pl.pallas_call(
        _final_norm_kernel,
        grid=(n // TM,),
        in_specs=[pl.BlockSpec((TM, D_MODEL), lambda i: (i, 0)), pl.BlockSpec((1, D_MODEL), lambda i: (0, 0))],
        out_specs=pl.BlockSpec((TM, D_MODEL), lambda i: (i, 0)),
        out_shape=jax.ShapeDtypeStruct((n, D_MODEL), F32),
        compiler_params=_cparams(("arbitrary",)),
        name="final_norm",
    )(x, g.reshape(1, D_MODEL))


def _attn_mixer(h1, x, mod5, layer, j, attn_w_qkv, attn_sink, attn_w_o, rope_cos, rope_sin, need_ctx):
    q_cols = A_HQ * A_DH
    epi = functools.partial(_epi_rope, rope_cols=(0, q_cols + A_HKV * A_DH), scale_cols=q_cols, scale=A_SCALE)
    extras = [(rope_cos, _rope_spec(TM_PROJ)), (rope_sin, _rope_spec(TM_PROJ))]
    qkv = _proj(h1, 0, attn_w_qkv, j, q_cols + 2 * A_HKV * A_DH, 512, epi, extras, BF16, "attn_qkv_proj")
    o = _gqa_attention(qkv, attn_sink[j].astype(F32), need_ctx)
    return _out_proj_resid(o, attn_w_o, j, x, mod5, layer, "attn_out_proj")


def _fnet_mixer(h1, x, mod5, layer, j, fnet_w_o, need_ctx):
    cc, sc = _dft_mats(F_GROUP_DIM, F_GROUP_DIM ** -0.5)
    w1 = jnp.concatenate([cc, sc], axis=1).astype(BF16)
    outs = []
    for seq_len, row_blk0, active in ((SEQ, 0, True), (CTX_LEN, N_LAT // min(TM, CTX_LEN), need_ctx)):
        if not active:
            continue
        cl, sl = _dft_mats(seq_len, seq_len ** -0.5)
        a2 = jnp.concatenate([cl, -sl], axis=1).astype(BF16)
        z = _fnet_stage1(h1, w1, row_blk0, seq_len, BATCH)
        y = _fnet_stage2(a2, z, seq_len, BATCH)
        outs.append(y.reshape(BATCH * seq_len, D_MODEL))
    y_all = jnp.concatenate(outs, axis=0) if len(outs) > 1 else outs[0]
    return _out_proj_resid(y_all, fnet_w_o, j, x, mod5, layer, "fnet_out_proj")


def _mla_mixer(h1, x, mod5, layer, j, mla_w_in, mla_g_q, mla_g_kv, mla_w_uq, mla_w_ukv, mla_w_o,
               rope_cos, rope_sin, need_ctx):
    g_cat = jnp.concatenate([mla_g_q[j], mla_g_kv[j]]).reshape(1, M_QR + M_KVR)
    extras = [(g_cat, pl.BlockSpec((1, M_QR), lambda jj, i, k: (0, jj)))]
    cqkv = _proj(h1, 0, mla_w_in, j, M_QR + M_KVR, M_QR, _epi_rmsnorm, extras, BF16, "mla_in_proj")
    w_kr = jnp.pad(mla_w_in[j][:, M_QR + M_KVR:], ((0, 0), (0, LANES - M_ROPE)))
    epi_kr = functools.partial(_epi_rope, rope_cols=(0, LANES), scale_cols=0, scale=1.0)
    rope_extras = [(rope_cos, _rope_spec(TM_PROJ)), (rope_sin, _rope_spec(TM_PROJ))]
    kr = _proj(h1, 0, w_kr, 0, LANES, LANES, epi_kr, rope_extras, BF16, "mla_kr_proj")
    w_uq = mla_w_uq[j].reshape(M_QR, M_H, M_NOPE + M_ROPE)
    w_qn = w_uq[:, :, :M_NOPE].reshape(M_QR, M_H * M_NOPE)
    w_qr = jnp.pad(w_uq[:, :, M_NOPE:], ((0, 0), (0, 0), (0, LANES - M_ROPE))).reshape(M_QR, M_H * LANES)
    w_q = jnp.concatenate([w_qn, w_qr], axis=1)
    nope_cols = M_H * M_NOPE
    epi_q = functools.partial(_epi_rope, rope_cols=(nope_cols, nope_cols + M_H * LANES),
                              scale_cols=nope_cols + M_H * LANES, scale=M_SCALE)
    qcat = _proj(cqkv, 0, w_q, 0, nope_cols + M_H * LANES, 1024, epi_q, rope_extras, BF16, "mla_q_proj")
    kv = _proj(cqkv, 1, mla_w_ukv, j, M_H * (M_NOPE + M_V), 1024, _epi_plain, [], BF16, "mla_kv_proj")
    o = _mla_attention(qcat, kv, kr, latent=True)
    if need_ctx:
        o = jnp.concatenate([o, _mla_attention(qcat, kv, kr, latent=False)], axis=0)
    return _out_proj_resid(o, mla_w_o, j, x, mod5, layer, "mla_out_proj")


def kernel(x, c, ctx, c_ctx, ada_w, ada_b, norm_g, final_g, attn_w_qkv, attn_sink, attn_w_o,
           fnet_w_o, mla_w_in, mla_g_q, mla_g_kv, mla_w_uq, mla_w_ukv, mla_w_o,
           router_w, router_b, moe_w_gate, moe_w_up, moe_w_down):
    xt = jnp.concatenate([x.reshape(N_LAT, D_MODEL), ctx.reshape(N_CTX, D_MODEL)], axis=0)
    src8 = jnp.concatenate([c, c_ctx[None], jnp.zeros((8 - BATCH - 1, D_MODEL), F32)], axis=0)
    mod5 = _ada_all(src8, ada_w, ada_b).reshape(DEPTH, 8, 6, 1, D_MODEL)
    norm_g4 = norm_g.reshape(DEPTH, 2, 1, D_MODEL)
    router_wt = router_w.T
    rope_cos, rope_sin = _rope_tables(TM_PROJ)
    for i in range(DEPTH):
        need_ctx = i < DEPTH - 1
        kind, j = i % N_MIXERS, i // N_MIXERS
        h1 = _norm1(xt, norm_g4, mod5, i)
        if kind == 0:
            x_mix = _attn_mixer(h1, xt, mod5, i, j, attn_w_qkv, attn_sink, attn_w_o, rope_cos, rope_sin, need_ctx)
        elif kind == 1:
            x_mix = _fnet_mixer(h1, xt, mod5, i, j, fnet_w_o, need_ctx)
        else:
            x_mix = _mla_mixer(h1, xt, mod5, i, j, mla_w_in, mla_g_q, mla_g_kv, mla_w_uq, mla_w_ukv, mla_w_o,
                               rope_cos, rope_sin, need_ctx)
        xt = _moe_layer(x_mix, norm_g4, mod5, router_wt, router_b, moe_w_gate, moe_w_up, moe_w_down, i)
    return _final_norm(xt[:N_LAT], final_g).reshape(BATCH, SEQ, D_MODEL)
```

```python
import functools
import math

import jax
import jax.numpy as jnp
import numpy as np
from jax import lax
from jax.experimental import pallas as pl
from jax.experimental.pallas import tpu as pltpu

D_MODEL = 2048
BATCH = 2
SEQ = 4096
DEPTH = 4
GRID_W = 64
CTX_LEN = 256
N_MIXERS = 3

BLK = 128
A_HQ = 32
A_HKV = 4
A_GROUPS = A_HQ // A_HKV
A_DH = 64
A_SCALE = A_DH ** -0.5

F_GROUPS = 8
F_GROUP_DIM = D_MODEL // F_GROUPS

M_H = 16
M_QR = 512
M_KVR = 512
M_NOPE = 128
M_ROPE = 64
M_V = 128
M_SCALE = (M_NOPE + M_ROPE) ** -0.5

N_EXPERTS = 16
N_EXPERT_GROUPS = 4
EXPERTS_PER_GROUP = N_EXPERTS // N_EXPERT_GROUPS
D_FF = 1408

ROPE_THETA = 10000.0
NORM_EPS = 1e-6
NEG_INF = -1e30

N_LAT = BATCH * SEQ
N_CTX = BATCH * CTX_LEN
N_TOK = N_LAT + N_CTX

LANES = 128
TM = 512
TM_PROJ = 1024
ROW_BLK = 256
ITEM_ROWS = 2048
N_ITEMS = N_EXPERTS + (2 * N_TOK) // ITEM_ROWS
FF_TILE = 128
DOWN_TILE = 256
VMEM_LIMIT = 56 * 1024 * 1024

F32 = jnp.float32
BF16 = jnp.bfloat16


def _cparams(sem):
    return pltpu.CompilerParams(dimension_semantics=sem, vmem_limit_bytes=VMEM_LIMIT)


def _row_group(i, n_lat_tiles, tiles_per_batch):
    return jnp.where(i < n_lat_tiles, i // tiles_per_batch, BATCH)


def _ada_kernel(src_ref, w_ref, b_ref, o_ref):
    s = src_ref[...]
    s = (s * jax.nn.sigmoid(s)).astype(BF16)
    acc = jnp.dot(s, w_ref[...].astype(BF16), preferred_element_type=F32)
    o_ref[...] = acc + b_ref[...]


def _ada_all(src8, ada_w, ada_b):
    tn = 1024
    n6 = ada_w.shape[-1]
    return pl.pallas_call(
        _ada_kernel,
        grid=(DEPTH, n6 // tn),
        in_specs=[
            pl.BlockSpec((8, D_MODEL), lambda l, j: (0, 0)),
            pl.BlockSpec((None, D_MODEL, tn), lambda l, j: (l, 0, j)),
            pl.BlockSpec((None, 1, tn), lambda l, j: (l, 0, j)),
        ],
        out_specs=pl.BlockSpec((None, 8, tn), lambda l, j: (l, 0, j)),
        out_shape=jax.ShapeDtypeStruct((DEPTH, 8, n6), F32),
        compiler_params=_cparams(("arbitrary", "arbitrary")),
        name="ada_ln",
    )(src8, ada_w, ada_b.reshape(DEPTH, 1, n6))


def _mod_spec(layer, which, n_lat_tiles, tiles_per_batch, ngrid):
    def imap(*idx):
        return (layer, _row_group(idx[0], n_lat_tiles, tiles_per_batch), which, 0, 0)
    del ngrid
    return pl.BlockSpec((None, None, None, 1, D_MODEL), imap)


def _norm_mod(x, g, sh, sc):
    y = x * lax.rsqrt(jnp.mean(x * x, axis=-1, keepdims=True) + NORM_EPS) * g
    return y * (1.0 + sc) + sh


def _norm1_kernel(x_ref, g_ref, sh_ref, sc_ref, o_ref):
    o_ref[...] = _norm_mod(x_ref[...], g_ref[...], sh_ref[...], sc_ref[...]).astype(o_ref.dtype)


def _norm1(x, norm_g4, mod5, layer):
    n = x.shape[0]
    nlt, tpb = N_LAT // TM, SEQ // TM
    return pl.pallas_call(
        _norm1_kernel,
        grid=(n // TM,),
        in_specs=[
            pl.BlockSpec((TM, D_MODEL), lambda i: (i, 0)),
            pl.BlockSpec((None, None, 1, D_MODEL), lambda i: (layer, 0, 0, 0)),
            _mod_spec(layer, 0, nlt, tpb, 1),
            _mod_spec(layer, 1, nlt, tpb, 1),
        ],
        out_specs=pl.BlockSpec((TM, D_MODEL), lambda i: (i, 0)),
        out_shape=jax.ShapeDtypeStruct((n, D_MODEL), BF16),
        compiler_params=_cparams(("arbitrary",)),
        name="norm1",
    )(x, norm_g4, mod5, mod5)


def _norm2_kernel(x_ref, g_ref, sh_ref, sc_ref, rw_ref, hp_ref, lg_ref):
    h = _norm_mod(x_ref[...], g_ref[...], sh_ref[...], sc_ref[...])
    lg_ref[...] = lax.dot_general(rw_ref[...], h, (((1,), (1,)), ((), ())),
                                  precision=lax.Precision.HIGHEST, preferred_element_type=F32)
    half = D_MODEL // 2
    hp_ref[...] = pltpu.pack_elementwise([h[:, :half], h[:, half:]], packed_dtype=BF16)


def _norm2(x, norm_g4, mod5, router_wt, layer):
    n = x.shape[0]
    nlt, tpb = N_LAT // TM, SEQ // TM
    return pl.pallas_call(
        _norm2_kernel,
        grid=(n // TM,),
        in_specs=[
            pl.BlockSpec((TM, D_MODEL), lambda i: (i, 0)),
            pl.BlockSpec((None, None, 1, D_MODEL), lambda i: (layer, 1, 0, 0)),
            _mod_spec(layer, 3, nlt, tpb, 1),
            _mod_spec(layer, 4, nlt, tpb, 1),
            pl.BlockSpec((N_EXPERTS, D_MODEL), lambda i: (0, 0)),
        ],
        out_specs=[
            pl.BlockSpec((TM, D_MODEL // 2), lambda i: (i, 0)),
            pl.BlockSpec((N_EXPERTS, TM), lambda i: (0, i)),
        ],
        out_shape=[
            jax.ShapeDtypeStruct((n, D_MODEL // 2), jnp.uint32),
            jax.ShapeDtypeStruct((N_EXPERTS, n), F32),
        ],
        compiler_params=_cparams(("arbitrary",)),
        name="norm2_router_logits",
    )(x, norm_g4, mod5, mod5, router_wt)


def _mm_kernel(*refs, nk, n_extra, epilogue):
    a_ref, w_ref = refs[0], refs[1]
    extra = refs[2:2 + n_extra]
    o_ref = refs[2 + n_extra]
    part = jnp.dot(a_ref[...].astype(BF16), w_ref[...].astype(BF16), preferred_element_type=F32)
    if nk == 1:
        epilogue(part, o_ref, *extra)
        return
    acc_ref = refs[3 + n_extra]
    k = pl.program_id(2)

    @pl.when(k == 0)
    def _():
        acc_ref[...] = part

    @pl.when(k > 0)
    def _():
        acc_ref[...] += part

    @pl.when(k == nk - 1)
    def _():
        epilogue(acc_ref[...], o_ref, *extra)


def _matmul(a, a_spec, w, w_spec, extras, out_shape, out_spec, grid, epilogue, tm, tn, name):
    nk = grid[2]
    kern = functools.partial(_mm_kernel, nk=nk, n_extra=len(extras), epilogue=epilogue)
    scratch = [pltpu.VMEM((tm, tn), F32)] if nk > 1 else []
    return pl.pallas_call(
        kern,
        grid=grid,
        in_specs=[a_spec, w_spec] + [s for _, s in extras],
        out_specs=out_spec,
        out_shape=out_shape,
        scratch_shapes=scratch,
        compiler_params=_cparams(("arbitrary", "arbitrary", "arbitrary")),
        name=name,
    )(a, w, *[x for x, _ in extras])


def _epi_plain(acc, o_ref):
    o_ref[...] = acc.astype(o_ref.dtype)


def _epi_resid_gate(acc, o_ref, x_ref, gate_ref):
    o_ref[...] = x_ref[...] + gate_ref[...] * acc


def _epi_rmsnorm(acc, o_ref, g_ref):
    y = acc * lax.rsqrt(jnp.mean(acc * acc, axis=-1, keepdims=True) + NORM_EPS) * g_ref[...]
    o_ref[...] = y.astype(o_ref.dtype)


def _rope_rotate(acc, cos_ref, sin_ref):
    tn = acc.shape[1]
    reps = tn // LANES
    cos = jnp.concatenate([cos_ref[...]] * reps, axis=1) if reps > 1 else cos_ref[...]
    sin = jnp.concatenate([sin_ref[...]] * reps, axis=1) if reps > 1 else sin_ref[...]
    up = pltpu.roll(acc, tn - 16, 1)
    dn = pltpu.roll(acc, 16, 1)
    lane = lax.broadcasted_iota(jnp.int32, acc.shape, 1)
    rot = jnp.where((lane % 32) < 16, up, dn)
    return acc * cos + rot * sin


def _epi_rope(acc, o_ref, cos_ref, sin_ref, *, rope_cols, scale_cols, scale):
    tn = acc.shape[1]
    col = pl.program_id(0) * tn + lax.broadcasted_iota(jnp.int32, acc.shape, 1)
    r = _rope_rotate(acc, cos_ref, sin_ref)
    r = jnp.where((col >= rope_cols[0]) & (col < rope_cols[1]), r, acc)
    if scale_cols:
        r = jnp.where(col < scale_cols, r * scale, r)
    o_ref[...] = r.astype(o_ref.dtype)


def _rope_tables(tm):
    d_axis = A_DH // 2
    inv_freq = ROPE_THETA ** (-jnp.arange(0, d_axis, 2, dtype=F32) / d_axis)
    t = jnp.arange(SEQ, dtype=jnp.int32)
    ang_r = (t // GRID_W).astype(F32)[:, None] * inv_freq
    ang_c = (t % GRID_W).astype(F32)[:, None] * inv_freq
    ang = jnp.concatenate([ang_r, ang_r, ang_c, ang_c] * 2, axis=1)
    sign = jnp.tile(jnp.concatenate([-jnp.ones(16, F32), jnp.ones(16, F32)]), 4)
    cos = jnp.concatenate([jnp.cos(ang), jnp.ones((tm, LANES), F32)], axis=0)
    sin = jnp.concatenate([jnp.sin(ang) * sign, jnp.zeros((tm, LANES), F32)], axis=0)
    return cos, sin


def _rope_spec(tm):
    nlt, tpb = N_LAT // tm, SEQ // tm
    return pl.BlockSpec((tm, LANES), lambda j, i, k: (jnp.where(i < nlt, i % tpb, tpb), 0))


def _proj(a, a_col, w, layer, n_out, tn, epilogue, extras, out_dtype, name):
    n = a.shape[0]
    kdim = w.shape[-2]
    grid = (n_out // tn, pl.cdiv(n, TM_PROJ), 1)
    if w.ndim == 3:
        w_spec = pl.BlockSpec((None, kdim, tn), lambda j, i, k: (layer, 0, j))
    else:
        w_spec = pl.BlockSpec((kdim, tn), lambda j, i, k: (0, j))
    return _matmul(
        a, pl.BlockSpec((TM_PROJ, kdim), lambda j, i, k: (i, a_col)), w, w_spec, extras,
        jax.ShapeDtypeStruct((n, n_out), out_dtype), pl.BlockSpec((TM_PROJ, tn), lambda j, i, k: (i, j)),
        grid, epilogue, TM_PROJ, tn, name)


def _out_proj_resid(o, w3, layer_w, x, mod5, layer, name):
    nlt, tpb = N_LAT // TM_PROJ, SEQ // TM_PROJ
    tn = 1024
    gate_spec = pl.BlockSpec(
        (None, None, None, 1, tn),
        lambda j, i, k: (layer, _row_group(i, nlt, tpb), 2, 0, j))
    extras = [(x, pl.BlockSpec((TM_PROJ, tn), lambda j, i, k: (i, j))), (mod5, gate_spec)]
    return _proj(o, 0, w3, layer_w, D_MODEL, tn, _epi_resid_gate, extras, F32, name)


def _dup_half(t, upper):
    h = t[:, A_DH:] if upper else t[:, :A_DH]
    return jnp.concatenate([h, h], axis=1)


def _gqa_kernel(sink_ref, q_ref, kp_ref, ks_ref, kn_ref, vp_ref, vs_ref, vn_ref, kc_ref, vc_ref, o_ref,
                *, n_lat_blocks, blocks_per_seq):
    n = pl.program_id(0)
    is_lat = n < n_lat_blocks
    nb = n % blocks_per_seq
    iq = lax.broadcasted_iota(jnp.int32, (BLK, 3 * BLK), 0)
    ik = lax.broadcasted_iota(jnp.int32, (BLK, 3 * BLK), 1)
    mask = (ik >= iq) & (ik <= iq + 2 * BLK)
    mask = mask & ((nb > 0) | (ik >= BLK)) & ((nb < blocks_per_seq - 1) | (ik < 2 * BLK)) & is_lat
    lane = lax.broadcasted_iota(jnp.int32, (BLK, LANES), 1)
    lo_half = lane < A_DH
    keep_lo = lo_half.astype(F32).astype(BF16)
    keep_hi = (1.0 - lo_half.astype(F32)).astype(BF16)
    n_pairs = A_GROUPS // 2
    for hk in range(A_HKV):
        c0 = (hk // 2) * LANES
        upper = (hk % 2) == 1
        k_all = jnp.concatenate([
            _dup_half(kp_ref[:, c0:c0 + LANES], upper), _dup_half(ks_ref[:, c0:c0 + LANES], upper),
            _dup_half(kn_ref[:, c0:c0 + LANES], upper), _dup_half(kc_ref[:, c0:c0 + LANES], upper)], axis=0)
        v_all = jnp.concatenate([
            _dup_half(vp_ref[:, c0:c0 + LANES], upper), _dup_half(vs_ref[:, c0:c0 + LANES], upper),
            _dup_half(vn_ref[:, c0:c0 + LANES], upper), _dup_half(vc_ref[:, c0:c0 + LANES], upper)], axis=0)
        ones = jnp.ones(v_all.shape, BF16)
        v_lo = v_all * keep_lo[0:1] + ones * keep_hi[0:1]
        v_hi = v_all * keep_hi[0:1] + ones * keep_lo[0:1]
        lhs = []
        for p in range(n_pairs):
            qp = q_ref[:, hk * A_GROUPS * A_DH + p * LANES: hk * A_GROUPS * A_DH + (p + 1) * LANES]
            lhs.append(qp * keep_lo)
            lhs.append(qp * keep_hi)
        lhs = jnp.concatenate(lhs, axis=0)
        s = lax.dot_general(lhs, k_all, (((1,), (1,)), ((), ())), preferred_element_type=F32)
        probs, sink_terms = [], []
        for g in range(A_GROUPS):
            sg = s[g * BLK:(g + 1) * BLK]
            s_loc = jnp.where(mask, sg[:, :3 * BLK], NEG_INF)
            s_ctx = sg[:, 3 * BLK:]
            sink = sink_ref[hk * A_GROUPS + g]
            m = jnp.maximum(jnp.maximum(jnp.max(s_loc, axis=-1, keepdims=True),
                                        jnp.max(s_ctx, axis=-1, keepdims=True)), sink)
            sink_terms.append(jnp.exp(sink - m))
            probs.append(jnp.exp(jnp.concatenate([s_loc - m, s_ctx - m], axis=1).astype(BF16)))
        res_lo = jnp.dot(jnp.concatenate(probs[0::2], axis=0), v_lo, preferred_element_type=F32)
        res_hi = jnp.dot(jnp.concatenate(probs[1::2], axis=0), v_hi, preferred_element_type=F32)
        for p in range(n_pairs):
            r_lo = res_lo[p * BLK:(p + 1) * BLK]
            r_hi = res_hi[p * BLK:(p + 1) * BLK]
            o_lo = r_lo / (r_lo[:, A_DH:A_DH + 1] + sink_terms[2 * p])
            o_hi = r_hi / (r_hi[:, 0:1] + sink_terms[2 * p + 1])
            col = hk * A_GROUPS * A_DH + p * LANES
            o_ref[:, col:col + LANES] = jnp.where(lo_half, o_lo, o_hi).astype(o_ref.dtype)


def _gqa_attention(qkv, sink, need_ctx):
    bps = SEQ // BLK
    nlb = BATCH * bps
    n_ctx_blocks = N_CTX // BLK if need_ctx else 0
    nq = nlb + n_ctx_blocks
    kcol, vcol = (A_HQ * A_DH) // 256, (A_HQ * A_DH) // 256 + 1
    cpb = CTX_LEN // BLK

    def batch_of(n):
        return jnp.where(n < nlb, n // bps, (n - nlb) // cpb)

    def loc_map(delta, col):
        def imap(n, s):
            nb = n % bps
            blk = jnp.clip(nb + delta, 0, bps - 1)
            return (jnp.where(n < nlb, (n // bps) * bps + blk, 0), col)
        return imap

    def ctx_map(col):
        return lambda n, s: (N_LAT // CTX_LEN + batch_of(n), col)

    kv_blk = (BLK, 256)
    grid_spec = pltpu.PrefetchScalarGridSpec(
        num_scalar_prefetch=1,
        grid=(nq,),
        in_specs=[
            pl.BlockSpec((BLK, A_HQ * A_DH), lambda n, s: (n, 0)),
            pl.BlockSpec(kv_blk, loc_map(-1, kcol)), pl.BlockSpec(kv_blk, loc_map(0, kcol)),
            pl.BlockSpec(kv_blk, loc_map(1, kcol)),
            pl.BlockSpec(kv_blk, loc_map(-1, vcol)), pl.BlockSpec(kv_blk, loc_map(0, vcol)),
            pl.BlockSpec(kv_blk, loc_map(1, vcol)),
            pl.BlockSpec((CTX_LEN, 256), ctx_map(kcol)), pl.BlockSpec((CTX_LEN, 256), ctx_map(vcol)),
        ],
        out_specs=pl.BlockSpec((BLK, A_HQ * A_DH), lambda n, s: (n, 0)),
    )
    kern = functools.partial(_gqa_kernel, n_lat_blocks=nlb, blocks_per_seq=bps)
    return pl.pallas_call(
        kern,
        grid_spec=grid_spec,
        out_shape=jax.ShapeDtypeStruct((nq * BLK, A_HQ * A_DH), BF16),
        compiler_params=_cparams(("arbitrary",)),
        name="gqa_window_attn",
    )(sink, qkv, qkv, qkv, qkv, qkv, qkv, qkv, qkv, qkv)


MLA_TQ = 256
MLA_TK = 512


def _mla_kernel(*refs, n_lat_keys):
    if n_lat_keys:
        (qn_ref, qr_ref, kn_ref, v_ref, kr_ref, knc_ref, vc_ref, krc_ref, o_ref, kcat, vcat) = refs
    else:
        (qn_ref, qr_ref, knc_ref, vc_ref, krc_ref, o_ref, kcat, vcat) = refs
    qi = pl.program_id(2)
    nk = n_lat_keys + CTX_LEN

    @pl.when(qi == 0)
    def _():
        if n_lat_keys:
            kcat[0:n_lat_keys, 0:LANES] = kn_ref[...]
            kcat[0:n_lat_keys, LANES:2 * LANES] = kr_ref[...]
            vcat[0:n_lat_keys, 0:M_V] = v_ref[...]
        kcat[n_lat_keys:nk, 0:LANES] = knc_ref[...]
        kcat[n_lat_keys:nk, LANES:2 * LANES] = krc_ref[...]
        vcat[n_lat_keys:nk, 0:M_V] = vc_ref[...]
        vcat[:, M_V:M_V + LANES] = jnp.ones((nk, LANES), BF16)

    q = jnp.concatenate([qn_ref[...], qr_ref[...]], axis=1)

    def chunk(k0, size):
        kc = kcat[k0:k0 + size, :]
        vc = vcat[k0:k0 + size, :]
        s = lax.dot_general(q, kc, (((1,), (1,)), ((), ())), preferred_element_type=F32)
        m_c = jnp.max(s, axis=-1, keepdims=True)
        p = jnp.exp((s - m_c).astype(BF16))
        ol = jnp.dot(p, vc, preferred_element_type=F32)
        return m_c, ol[:, M_V:M_V + 1], ol[:, :M_V]

    parts = [chunk(c * MLA_TK, MLA_TK) for c in range(n_lat_keys // MLA_TK)]
    parts.append(chunk(n_lat_keys, CTX_LEN))
    m = parts[0][0]
    for m_c, _, _ in parts[1:]:
        m = jnp.maximum(m, m_c)
    l = None
    acc = None
    for m_c, l_c, o_c in parts:
        w_c = jnp.exp(m_c - m)
        l = w_c * l_c if l is None else l + w_c * l_c
        acc = w_c * o_c if acc is None else acc + w_c * o_c
    o_ref[...] = (acc / l).astype(o_ref.dtype)


def _mla_attention(qcat, kv, kr, latent):
    hq = M_H
    ctx_row_blk = N_LAT // CTX_LEN
    if latent:
        tq, nqt, n_lat_keys = MLA_TQ, SEQ // MLA_TQ, SEQ
        row0 = lambda b, qi: b * nqt + qi
        n_out = N_LAT
    else:
        tq, nqt, n_lat_keys = CTX_LEN, 1, 0
        row0 = lambda b, qi: ctx_row_blk + b
        n_out = N_CTX
    in_specs = [
        pl.BlockSpec((tq, LANES), lambda b, h, qi: (row0(b, qi), h)),
        pl.BlockSpec((tq, LANES), lambda b, h, qi: (row0(b, qi), hq + h)),
    ]
    args = [qcat, qcat]
    if latent:
        in_specs += [
            pl.BlockSpec((SEQ, LANES), lambda b, h, qi: (b, 2 * h)),
            pl.BlockSpec((SEQ, LANES), lambda b, h, qi: (b, 2 * h + 1)),
            pl.BlockSpec((SEQ, LANES), lambda b, h, qi: (b, 0)),
        ]
        args += [kv, kv, kr]
    in_specs += [
        pl.BlockSpec((CTX_LEN, LANES), lambda b, h, qi: (ctx_row_blk + b, 2 * h)),
        pl.BlockSpec((CTX_LEN, LANES), lambda b, h, qi: (ctx_row_blk + b, 2 * h + 1)),
        pl.BlockSpec((CTX_LEN, LANES), lambda b, h, qi: (ctx_row_blk + b, 0)),
    ]
    args += [kv, kv, kr]
    if latent:
        out_spec = pl.BlockSpec((tq, LANES), lambda b, h, qi: (b * nqt + qi, h))
    else:
        out_spec = pl.BlockSpec((tq, LANES), lambda b, h, qi: (b, h))
    nk = n_lat_keys + CTX_LEN
    return pl.pallas_call(
        functools.partial(_mla_kernel, n_lat_keys=n_lat_keys),
        grid=(BATCH, hq, nqt),
        in_specs=in_specs,
        out_specs=out_spec,
        out_shape=jax.ShapeDtypeStruct((n_out, hq * M_V), BF16),
        scratch_shapes=[pltpu.VMEM((nk, 2 * LANES), BF16), pltpu.VMEM((nk, M_V + LANES), BF16)],
        compiler_params=_cparams(("arbitrary", "arbitrary", "arbitrary")),
        name="mla_attn_lat" if latent else "mla_attn_ctx",
    )(*args)


def _dft_mats(n, scale):
    sub = 64
    k = jnp.arange(n, dtype=jnp.int32)
    j1 = jnp.arange(n // sub, dtype=jnp.int32) * sub
    j0 = jnp.arange(sub, dtype=jnp.int32)
    a = ((j1[:, None] * k[None, :]) % n).astype(F32) * (2.0 * math.pi / n)
    b = ((j0[:, None] * k[None, :]) % n).astype(F32) * (2.0 * math.pi / n)
    ca, sa = (jnp.cos(a) * scale)[:, None, :], (jnp.sin(a) * scale)[:, None, :]
    cb, sb = jnp.cos(b)[None, :, :], jnp.sin(b)[None, :, :]
    return (ca * cb - sa * sb).reshape(n, n), (sa * cb + ca * sb).reshape(n, n)


def _fnet_stage1_kernel(h_ref, w_ref, z_ref):
    z = jnp.dot(h_ref[...], w_ref[...], preferred_element_type=F32)
    z_ref[0] = z[:, :F_GROUP_DIM].astype(z_ref.dtype)
    z_ref[1] = z[:, F_GROUP_DIM:].astype(z_ref.dtype)


def _fnet_stage1(h1, w1, row_blk0, seq_len, nbatch):
    tm = min(TM, seq_len)
    tiles = seq_len // tm
    return pl.pallas_call(
        _fnet_stage1_kernel,
        grid=(nbatch, tiles, F_GROUPS),
        in_specs=[
            pl.BlockSpec((tm, F_GROUP_DIM), lambda b, i, g: (row_blk0 + b * tiles + i, g)),
            pl.BlockSpec((F_GROUP_DIM, 2 * F_GROUP_DIM), lambda b, i, g: (0, 0)),
        ],
        out_specs=pl.BlockSpec((None, 2, tm, F_GROUP_DIM), lambda b, i, g: (b, 0, i, g)),
        out_shape=jax.ShapeDtypeStruct((nbatch, 2, seq_len, D_MODEL), BF16),
        compiler_params=_cparams(("arbitrary", "arbitrary", "arbitrary")),
        name="fnet_channel_dft",
    )(h1, w1)


def _fnet_stage2_kernel(a_ref, z_ref, o_ref, acc_ref, *, nk):
    k = pl.program_id(2)
    part = jnp.dot(a_ref[...], z_ref[...], preferred_element_type=F32)

    @pl.when(k == 0)
    def _():
        acc_ref[...] = part

    @pl.when(k > 0)
    def _():
        acc_ref[...] += part

    @pl.when(k == nk - 1)
    def _():
        o_ref[...] = acc_ref[...].astype(o_ref.dtype)


def _fnet_stage2(a2, z, seq_len, nbatch):
    tm = min(1024, seq_len)
    tk = min(512, seq_len)
    nk = 2 * seq_len // tk
    z2 = z.reshape(nbatch, 2 * seq_len, D_MODEL)
    return pl.pallas_call(
        functools.partial(_fnet_stage2_kernel, nk=nk),
        grid=(nbatch, seq_len // tm, nk),
        in_specs=[
            pl.BlockSpec((tm, tk), lambda b, i, k: (i, k)),
            pl.BlockSpec((None, tk, D_MODEL), lambda b, i, k: (b, k, 0)),
        ],
        out_specs=pl.BlockSpec((None, tm, D_MODEL), lambda b, i, k: (b, i, 0)),
        out_shape=jax.ShapeDtypeStruct((nbatch, seq_len, D_MODEL), BF16),
        scratch_shapes=[pltpu.VMEM((tm, D_MODEL), F32)],
        compiler_params=_cparams(("arbitrary", "arbitrary", "arbitrary")),
        name="fnet_position_dft",
    )(a2, z2)


ROUTER_TILE = 512


def _top2_rows(v, rowf):
    big = float(v.shape[0])
    m1 = jnp.max(v, axis=0, keepdims=True)
    i1 = jnp.min(jnp.where(v == m1, rowf, big), axis=0, keepdims=True)
    v2 = jnp.where(rowf == i1, NEG_INF, v)
    m2 = jnp.max(v2, axis=0, keepdims=True)
    i2 = jnp.min(jnp.where(v2 == m2, rowf, big), axis=0, keepdims=True)
    return m1, i1, m2, i2


def _router_kernel(lg_ref, b_ref, idx_ref, w_ref, cnt_ref, carry_ref):
    step = pl.program_id(0)
    t = lg_ref.shape[1]

    @pl.when(step == 0)
    def _():
        carry_ref[...] = jnp.zeros_like(carry_ref)

    scores = jax.nn.sigmoid(lg_ref[...])
    sel = scores + b_ref[...][:, 0:1]
    row = lax.broadcasted_iota(jnp.int32, (N_EXPERTS, t), 0)
    rowf = row.astype(F32)
    grp = row // EXPERTS_PER_GROUP
    best = jnp.zeros((1, t), jnp.int32)
    best_v = None
    for g in range(N_EXPERT_GROUPS):
        m1, _, m2, _ = _top2_rows(jnp.where(grp == g, sel, NEG_INF), rowf)
        gs = m1 + m2
        if g == 0:
            best_v = gs
        else:
            upd = gs > best_v
            best = jnp.where(upd, g, best)
            best_v = jnp.where(upd, gs, best_v)
    _, e0, _, e1 = _top2_rows(jnp.where(grp == best, sel, NEG_INF), rowf)
    oh0 = rowf == e0
    oh1 = rowf == e1
    s0 = jnp.sum(jnp.where(oh0, scores, 0.0), axis=0, keepdims=True)
    s1 = jnp.sum(jnp.where(oh1, scores, 0.0), axis=0, keepdims=True)
    tot = s0 + s1
    cnt = jnp.where(oh0 | oh1, 1.0, 0.0)
    a = lax.broadcasted_iota(jnp.int32, (t, t), 0)
    bcol = lax.broadcasted_iota(jnp.int32, (t, t), 1)
    tri = jnp.where(a < bcol, 1.0, 0.0).astype(BF16)
    excl = jnp.dot(cnt.astype(BF16), tri, preferred_element_type=F32) + carry_ref[...][:, 0:1]
    r0 = jnp.sum(jnp.where(oh0, excl, 0.0), axis=0, keepdims=True)
    r1 = jnp.sum(jnp.where(oh1, excl, 0.0), axis=0, keepdims=True)
    new_carry = carry_ref[...] + jnp.sum(cnt, axis=1, keepdims=True)
    carry_ref[...] = new_carry
    idx_ref[...] = jnp.zeros(idx_ref.shape, jnp.int32)
    idx_ref[0:1, :] = e0.astype(jnp.int32)
    idx_ref[1:2, :] = e1.astype(jnp.int32)
    idx_ref[2:3, :] = r0.astype(jnp.int32)
    idx_ref[3:4, :] = r1.astype(jnp.int32)
    w_ref[...] = jnp.zeros(w_ref.shape, F32)
    w_ref[0:1, :] = s0 / tot
    w_ref[1:2, :] = s1 / tot
    cnt_ref[...] = new_carry.astype(jnp.int32)


def _router(logits_t, router_b):
    n = logits_t.shape[1]
    t = ROUTER_TILE
    b2 = jnp.broadcast_to(router_b.astype(F32)[:, None], (N_EXPERTS, LANES))
    return pl.pallas_call(
        _router_kernel,
        grid=(n // t,),
        in_specs=[pl.BlockSpec((N_EXPERTS, t), lambda i: (0, i)),
                  pl.BlockSpec((N_EXPERTS, LANES), lambda i: (0, 0))],
        out_specs=[pl.BlockSpec((8, t), lambda i: (0, i)),
                   pl.BlockSpec((8, t), lambda i: (0, i)),
                   pl.BlockSpec((N_EXPERTS, LANES), lambda i: (0, 0))],
        out_shape=[jax.ShapeDtypeStruct((8, n), jnp.int32),
                   jax.ShapeDtypeStruct((8, n), F32),
                   jax.ShapeDtypeStruct((N_EXPERTS, LANES), jnp.int32)],
        scratch_shapes=[pltpu.VMEM((N_EXPERTS, LANES), F32)],
        compiler_params=_cparams(("arbitrary",)),
        name="moe_router",
    )(logits_t, b2)


def _plan_items(counts):
    n_chunks = (counts + ITEM_ROWS - 1) // ITEM_ROWS
    ends = jnp.cumsum(n_chunks)
    starts = ends - n_chunks
    base = starts * ITEM_ROWS
    n_valid = ends[-1]
    item = jnp.arange(N_ITEMS, dtype=jnp.int32)
    item_src = jnp.minimum(item, n_valid - 1)
    expert = jnp.sum((ends[None, :] <= item_src[:, None]).astype(jnp.int32), axis=1)
    rows = jnp.clip(counts[expert] - (item_src - starts[expert]) * ITEM_ROWS, 0, ITEM_ROWS)
    rows = jnp.where(item < n_valid, rows, 0).astype(jnp.int32)
    nblk = (rows + ROW_BLK - 1) // ROW_BLK
    return base.astype(jnp.int32), item_src.astype(jnp.int32), expert, nblk, rows


DISPATCH_TILE = 256


def _dispatch_kernel(p0_ref, p1_ref, hp_ref, hs_ref, sem):
    base = pl.program_id(0) * DISPATCH_TILE

    def copies(r):
        t = base + r
        return (pltpu.make_async_copy(hp_ref.at[pl.ds(r, 1)], hs_ref.at[pl.ds(p0_ref[t], 1)], sem),
                pltpu.make_async_copy(hp_ref.at[pl.ds(r, 1)], hs_ref.at[pl.ds(p1_ref[t], 1)], sem))

    def start(r, c):
        a, b = copies(r)
        a.start()
        b.start()
        return c

    def wait(r, c):
        a, b = copies(r)
        a.wait()
        b.wait()
        return c

    lax.fori_loop(0, DISPATCH_TILE, start, 0, unroll=8)
    lax.fori_loop(0, DISPATCH_TILE, wait, 0, unroll=8)


def _dispatch(hp, pos0, pos1):
    n = hp.shape[0]
    grid_spec = pltpu.PrefetchScalarGridSpec(
        num_scalar_prefetch=2,
        grid=(n // DISPATCH_TILE,),
        in_specs=[pl.BlockSpec((DISPATCH_TILE, D_MODEL // 2), lambda i, p0, p1: (i, 0))],
        out_specs=pl.BlockSpec(memory_space=pl.ANY),
        scratch_shapes=[pltpu.SemaphoreType.DMA(())],
    )
    return pl.pallas_call(
        _dispatch_kernel,
        grid_spec=grid_spec,
        out_shape=jax.ShapeDtypeStruct((N_ITEMS * ITEM_ROWS, D_MODEL // 2), hp.dtype),
        compiler_params=pltpu.CompilerParams(dimension_semantics=("arbitrary",), has_side_effects=True),
        name="moe_dispatch",
    )(pos0, pos1, hp)


def _gate_up_kernel(src_ref, exp_ref, nblk_ref, rows_ref, hs_ref, wg_ref, wu_ref, a_ref, hb_ref):
    del src_ref, exp_ref
    i = pl.program_id(0)
    f = pl.program_id(1)
    nblk = nblk_ref[i]
    n_rows = rows_ref[i]
    half = D_MODEL // 2

    @pl.when(nblk > 0)
    def _():
        @pl.when(f == 0)
        def _():
            def unpack(r, c):
                rows = pl.ds(pl.multiple_of(r * ROW_BLK, ROW_BLK), ROW_BLK)
                u = hs_ref[rows, :]
                live = (r * ROW_BLK + lax.broadcasted_iota(jnp.int32, (ROW_BLK, half), 0)) < n_rows
                lo = pltpu.unpack_elementwise(u, index=0, packed_dtype=BF16, unpacked_dtype=F32)
                hi = pltpu.unpack_elementwise(u, index=1, packed_dtype=BF16, unpacked_dtype=F32)
                hb_ref[rows, 0:half] = jnp.where(live, lo, 0.0).astype(BF16)
                hb_ref[rows, half:D_MODEL] = jnp.where(live, hi, 0.0).astype(BF16)
                return c
            lax.fori_loop(0, nblk, unpack, 0)

        for m in range(1, ITEM_ROWS // ROW_BLK + 1):
            rows = m * ROW_BLK

            @pl.when(nblk == m)
            def _():
                w = jnp.concatenate([wg_ref[...].astype(BF16), wu_ref[...].astype(BF16)], axis=1)
                res = jnp.dot(hb_ref[0:rows, :], w, preferred_element_type=F32)
                g = res[:, :FF_TILE]
                u = res[:, FF_TILE:]
                a_ref[0:rows, :] = (g * jax.nn.sigmoid(g) * u).astype(a_ref.dtype)
                if rows < ITEM_ROWS:
                    a_ref[rows:ITEM_ROWS, :] = jnp.zeros((ITEM_ROWS - rows, FF_TILE), a_ref.dtype)


def _gate_up(hs, w_gate, w_up, layer, item_src, expert, nblk, rows):
    nf = D_FF // FF_TILE

    def frozen_f(i, f, nb):
        return jnp.where(nb[i] > 0, f, nf - 1)

    grid_spec = pltpu.PrefetchScalarGridSpec(
        num_scalar_prefetch=4,
        grid=(N_ITEMS, nf),
        in_specs=[
            pl.BlockSpec((ITEM_ROWS, D_MODEL // 2), lambda i, f, src, ex, nb, rw: (src[i], 0)),
            pl.BlockSpec((None, None, D_MODEL, FF_TILE),
                         lambda i, f, src, ex, nb, rw: (layer, ex[i], 0, frozen_f(i, f, nb))),
            pl.BlockSpec((None, None, D_MODEL, FF_TILE),
                         lambda i, f, src, ex, nb, rw: (layer, ex[i], 0, frozen_f(i, f, nb))),
        ],
        out_specs=pl.BlockSpec((ITEM_ROWS, FF_TILE),
                               lambda i, f, src, ex, nb, rw: (src[i], frozen_f(i, f, nb))),
        scratch_shapes=[pltpu.VMEM((ITEM_ROWS, D_MODEL), BF16)],
    )
    return pl.pallas_call(
        _gate_up_kernel,
        grid_spec=grid_spec,
        out_shape=jax.ShapeDtypeStruct((N_ITEMS * ITEM_ROWS, D_FF), BF16),
        compiler_params=_cparams(("arbitrary", "arbitrary")),
        name="moe_gate_up",
    )(item_src, expert, nblk, rows, hs, w_gate, w_up)


def _down_kernel(src_ref, exp_ref, nblk_ref, a_ref, wlo_ref, whi_ref, y_ref):
    del src_ref, exp_ref
    i = pl.program_id(0)
    nblk = nblk_ref[i]

    for m in range(1, ITEM_ROWS // ROW_BLK + 1):
        rows = m * ROW_BLK

        @pl.when(nblk == m)
        def _():
            w = jnp.concatenate([wlo_ref[...].astype(BF16), whi_ref[...].astype(BF16)], axis=1)
            res = jnp.dot(a_ref[0:rows, :], w, preferred_element_type=F32)
            y_ref[0:rows, :] = pltpu.pack_elementwise([res[:, :DOWN_TILE], res[:, DOWN_TILE:]], packed_dtype=BF16)
            if rows < ITEM_ROWS:
                zero = jnp.zeros((ITEM_ROWS - rows, DOWN_TILE), F32)
                y_ref[rows:ITEM_ROWS, :] = pltpu.pack_elementwise([zero, zero], packed_dtype=BF16)


def _down(a, w_down, layer, item_src, expert, nblk):
    nj = D_MODEL // (2 * DOWN_TILE)

    def frozen_j(i, j, nb):
        return jnp.where(nb[i] > 0, j, nj - 1)

    grid_spec = pltpu.PrefetchScalarGridSpec(
        num_scalar_prefetch=3,
        grid=(N_ITEMS, nj),
        in_specs=[
            pl.BlockSpec((ITEM_ROWS, D_FF), lambda i, j, src, ex, nb: (src[i], 0)),
            pl.BlockSpec((None, None, D_FF, DOWN_TILE),
                         lambda i, j, src, ex, nb: (layer, ex[i], 0, frozen_j(i, j, nb))),
            pl.BlockSpec((None, None, D_FF, DOWN_TILE),
                         lambda i, j, src, ex, nb: (layer, ex[i], 0, nj + frozen_j(i, j, nb))),
        ],
        out_specs=pl.BlockSpec((ITEM_ROWS, DOWN_TILE), lambda i, j, src, ex, nb: (src[i], frozen_j(i, j, nb))),
    )
    return pl.pallas_call(
        _down_kernel,
        grid_spec=grid_spec,
        out_shape=jax.ShapeDtypeStruct((N_ITEMS * ITEM_ROWS, D_MODEL // 2), jnp.uint32),
        compiler_params=_cparams(("arbitrary", "arbitrary")),
        name="moe_down",
    )(item_src, expert, nblk, a, w_down, w_down)


COMBINE_TILE = 256


def _combine_kernel(p0_ref, p1_ref, x_ref, gate_ref, w0_ref, w1_ref, g_ref, sh_ref, sc_ref, ys_ref, *rest, final):
    if final:
        o_ref, y0_ref, y1_ref, sem = rest
    else:
        xo_ref, h_ref, y0_ref, y1_ref, sem = rest
    base = pl.program_id(0) * COMBINE_TILE

    def copies(r):
        t = base + r
        return (pltpu.make_async_copy(ys_ref.at[pl.ds(p0_ref[t], 1)], y0_ref.at[pl.ds(r, 1)], sem),
                pltpu.make_async_copy(ys_ref.at[pl.ds(p1_ref[t], 1)], y1_ref.at[pl.ds(r, 1)], sem))

    def start(r, c):
        a, b = copies(r)
        a.start()
        b.start()
        return c

    def wait(r, c):
        a, b = copies(r)
        a.wait()
        b.wait()
        return c

    lax.fori_loop(0, COMBINE_TILE, start, 0, unroll=8)
    lax.fori_loop(0, COMBINE_TILE, wait, 0, unroll=8)
    half = D_MODEL // 2
    w0 = w0_ref[...]
    w1 = w1_ref[...]
    xs = []
    for hh in range(2):
        cols = slice(hh * half, (hh + 1) * half)
        y = (w0 * pltpu.unpack_elementwise(y0_ref[...], index=hh, packed_dtype=BF16, unpacked_dtype=F32)
             + w1 * pltpu.unpack_elementwise(y1_ref[...], index=hh, packed_dtype=BF16, unpacked_dtype=F32))
        xs.append(x_ref[:, cols] + gate_ref[:, cols] * y)
    ssq = jnp.sum(xs[0] * xs[0], axis=-1, keepdims=True) + jnp.sum(xs[1] * xs[1], axis=-1, keepdims=True)
    inv = lax.rsqrt(ssq * (1.0 / D_MODEL) + NORM_EPS)
    for hh in range(2):
        cols = slice(hh * half, (hh + 1) * half)
        normed = xs[hh] * inv * g_ref[:, cols]
        if final:
            o_ref[:, cols] = normed
        else:
            xo_ref[:, cols] = xs[hh]
            h_ref[:, cols] = (normed * (1.0 + sc_ref[:, cols]) + sh_ref[:, cols]).astype(h_ref.dtype)


def _combine(x, ys, pos0, pos1, w0, w1, mod5, layer, norm_g, next_layer):
    n = x.shape[0]
    t = COMBINE_TILE
    nlt, tpb = N_LAT // t, SEQ // t
    final = next_layer is None
    mod_layer = layer if final else next_layer
    g_layer = 0 if final else next_layer

    def mod_spec(lyr, which):
        return pl.BlockSpec((None, None, None, 1, D_MODEL),
                            lambda i, p0, p1: (lyr, _row_group(i, nlt, tpb), which, 0, 0))

    row_spec = pl.BlockSpec((t, D_MODEL), lambda i, p0, p1: (i, 0))
    grid_spec = pltpu.PrefetchScalarGridSpec(
        num_scalar_prefetch=2,
        grid=(n // t,),
        in_specs=[
            row_spec,
            mod_spec(layer, 5),
            pl.BlockSpec((t, 1), lambda i, p0, p1: (i, 0)),
            pl.BlockSpec((t, 1), lambda i, p0, p1: (i, 0)),
            pl.BlockSpec((None, None, 1, D_MODEL), lambda i, p0, p1: (g_layer, 0, 0, 0)),
            mod_spec(mod_layer, 0),
            mod_spec(mod_layer, 1),
            pl.BlockSpec(memory_space=pl.ANY),
        ],
        out_specs=row_spec if final else [row_spec, row_spec],
        scratch_shapes=[pltpu.VMEM((t, D_MODEL // 2), jnp.uint32), pltpu.VMEM((t, D_MODEL // 2), jnp.uint32),
                        pltpu.SemaphoreType.DMA(())],
    )
    if final:
        out_shape = jax.ShapeDtypeStruct((n, D_MODEL), F32)
    else:
        out_shape = [jax.ShapeDtypeStruct((n, D_MODEL), F32), jax.ShapeDtypeStruct((n, D_MODEL), BF16)]
    return pl.pallas_call(
        functools.partial(_combine_kernel, final=final),
        grid_spec=grid_spec,
        out_shape=out_shape,
        compiler_params=_cparams(("arbitrary",)),
        name="moe_combine_final" if final else "moe_combine",
    )(pos0, pos1, x, mod5, w0, w1, norm_g, mod5, mod5, ys)


def _moe_layer(x, norm_g4, mod5, router_wt, router_b, w_gate, w_up, w_down, layer, out_norm_g, next_layer):
    hp, logits_t = _norm2(x, norm_g4, mod5, router_wt, layer)
    idx, wts, counts = _router(logits_t, router_b)
    base, item_src, expert, nblk, rows = _plan_items(counts[:, 0])
    pos0 = base[idx[0]] + idx[2]
    pos1 = base[idx[1]] + idx[3]
    hs = _dispatch(hp, pos0, pos1)
    a = _gate_up(hs, w_gate, w_up, layer, item_src, expert, nblk, rows)
    ys = _down(a, w_down, layer, item_src, expert, nblk)
    return _combine(x, ys, pos0, pos1, wts[0][:, None], wts[1][:, None], mod5, layer, out_norm_g, next_layer)


def _attn_mixer(h1, x, mod5, layer, j, attn_w_qkv, attn_sink, attn_w_o, rope_cos, rope_sin, need_ctx):
    q_cols = A_HQ * A_DH
    epi = functools.partial(_epi_rope, rope_cols=(0, q_cols + A_HKV * A_DH), scale_cols=q_cols, scale=A_SCALE)
    extras = [(rope_cos, _rope_spec(TM_PROJ)), (rope_sin, _rope_spec(TM_PROJ))]
    qkv = _proj(h1, 0, attn_w_qkv, j, q_cols + 2 * A_HKV * A_DH, 512, epi, extras, BF16, "attn_qkv_proj")
    o = _gqa_attention(qkv, attn_sink[j].astype(F32), need_ctx)
    return _out_proj_resid(o, attn_w_o, j, x, mod5, layer, "attn_out_proj")


def _fnet_mixer(h1, x, mod5, layer, j, fnet_w_o, need_ctx):
    cc, sc = _dft_mats(F_GROUP_DIM, F_GROUP_DIM ** -0.5)
    w1 = jnp.concatenate([cc, sc], axis=1).astype(BF16)
    outs = []
    for seq_len, row_blk0, active in ((SEQ, 0, True), (CTX_LEN, N_LAT // min(TM, CTX_LEN), need_ctx)):
        if not active:
            continue
        cl, sl = _dft_mats(seq_len, seq_len ** -0.5)
        a2 = jnp.concatenate([cl, -sl], axis=1).astype(BF16)
        z = _fnet_stage1(h1, w1, row_blk0, seq_len, BATCH)
        y = _fnet_stage2(a2, z, seq_len, BATCH)
        outs.append(y.reshape(BATCH * seq_len, D_MODEL))
    y_all = jnp.concatenate(outs, axis=0) if len(outs) > 1 else outs[0]
    return _out_proj_resid(y_all, fnet_w_o, j, x, mod5, layer, "fnet_out_proj")


def _mla_mixer(h1, x, mod5, layer, j, mla_w_in, mla_g_q, mla_g_kv, mla_w_uq, mla_w_ukv, mla_w_o,
               rope_cos, rope_sin, need_ctx):
    g_cat = jnp.concatenate([mla_g_q[j], mla_g_kv[j]]).reshape(1, M_QR + M_KVR)
    extras = [(g_cat, pl.BlockSpec((1, M_QR), lambda jj, i, k: (0, jj)))]
    cqkv = _proj(h1, 0, mla_w_in, j, M_QR + M_KVR, M_QR, _epi_rmsnorm, extras, BF16, "mla_in_proj")
    w_kr = jnp.pad(mla_w_in[j][:, M_QR + M_KVR:], ((0, 0), (0, LANES - M_ROPE)))
    epi_kr = functools.partial(_epi_rope, rope_cols=(0, LANES), scale_cols=0, scale=1.0)
    rope_extras = [(rope_cos, _rope_spec(TM_PROJ)), (rope_sin, _rope_spec(TM_PROJ))]
    kr = _proj(h1, 0, w_kr, 0, LANES, LANES, epi_kr, rope_extras, BF16, "mla_kr_proj")
    w_uq = mla_w_uq[j].reshape(M_QR, M_H, M_NOPE + M_ROPE)
    w_qn = w_uq[:, :, :M_NOPE].reshape(M_QR, M_H * M_NOPE)
    w_qr = jnp.pad(w_uq[:, :, M_NOPE:], ((0, 0), (0, 0), (0, LANES - M_ROPE))).reshape(M_QR, M_H * LANES)
    w_q = jnp.concatenate([w_qn, w_qr], axis=1)
    nope_cols = M_H * M_NOPE
    epi_q = functools.partial(_epi_rope, rope_cols=(nope_cols, nope_cols + M_H * LANES),
                              scale_cols=nope_cols + M_H * LANES, scale=M_SCALE)
    qcat = _proj(cqkv, 0, w_q, 0, nope_cols + M_H * LANES, 1024, epi_q, rope_extras, BF16, "mla_q_proj")
    kv = _proj(cqkv, 1, mla_w_ukv, j, M_H * (M_NOPE + M_V), 1024, _epi_plain, [], BF16, "mla_kv_proj")
    o = _mla_attention(qcat, kv, kr, latent=True)
    if need_ctx:
        o = jnp.concatenate([o, _mla_attention(qcat, kv, kr, latent=False)], axis=0)
    return _out_proj_resid(o, mla_w_o, j, x, mod5, layer, "mla_out_proj")


def kernel(x, c, ctx, c_ctx, ada_w, ada_b, norm_g, final_g, attn_w_qkv, attn_sink, attn_w_o,
           fnet_w_o, mla_w_in, mla_g_q, mla_g_kv, mla_w_uq, mla_w_ukv, mla_w_o,
           router_w, router_b, moe_w_gate, moe_w_up, moe_w_down):
    xt = jnp.concatenate([x.reshape(N_LAT, D_MODEL), ctx.reshape(N_CTX, D_MODEL)], axis=0)
    src8 = jnp.concatenate([c, c_ctx[None], jnp.zeros((8 - BATCH - 1, D_MODEL), F32)], axis=0)
    mod5 = _ada_all(src8, ada_w, ada_b).reshape(DEPTH, 8, 6, 1, D_MODEL)
    norm_g4 = norm_g.reshape(DEPTH, 2, 1, D_MODEL)
    router_wt = router_w.T
    rope_cos, rope_sin = _rope_tables(TM_PROJ)
    final_g4 = final_g.reshape(1, 1, 1, D_MODEL)
    h1 = _norm1(xt, norm_g4, mod5, 0)
    for i in range(DEPTH):
        need_ctx = i < DEPTH - 1
        kind, j = i % N_MIXERS, i // N_MIXERS
        if kind == 0:
            x_mix = _attn_mixer(h1, xt, mod5, i, j, attn_w_qkv, attn_sink, attn_w_o, rope_cos, rope_sin, need_ctx)
        elif kind == 1:
            x_mix = _fnet_mixer(h1, xt, mod5, i, j, fnet_w_o, need_ctx)
        else:
            x_mix = _mla_mixer(h1, xt, mod5, i, j, mla_w_in, mla_g_q, mla_g_kv, mla_w_uq, mla_w_ukv, mla_w_o,
                               rope_cos, rope_sin, need_ctx)
        moe_args = (x_mix, norm_g4, mod5, router_wt, router_b, moe_w_gate, moe_w_up, moe_w_down, i)
        if need_ctx:
            xt, h1 = _moe_layer(*moe_args, norm_g4, i + 1)
        else:
            out = _moe_layer(*moe_args, final_g4, None)
    return out.reshape(BATCH, SEQ, D_MODEL)
```

```python
import functools
import math

import jax
import jax.numpy as jnp
import numpy as np
from jax import lax
from jax.experimental import pallas as pl
from jax.experimental.pallas import tpu as pltpu

D_MODEL = 2048
BATCH = 2
SEQ = 4096
DEPTH = 4
GRID_W = 64
CTX_LEN = 256
N_MIXERS = 3

BLK = 128
A_HQ = 32
A_HKV = 4
A_GROUPS = A_HQ // A_HKV
A_DH = 64
A_SCALE = A_DH ** -0.5

F_GROUPS = 8
F_GROUP_DIM = D_MODEL // F_GROUPS

M_H = 16
M_QR = 512
M_KVR = 512
M_NOPE = 128
M_ROPE = 64
M_V = 128
M_SCALE = (M_NOPE + M_ROPE) ** -0.5

N_EXPERTS = 16
N_EXPERT_GROUPS = 4
EXPERTS_PER_GROUP = N_EXPERTS // N_EXPERT_GROUPS
D_FF = 1408

ROPE_THETA = 10000.0
NORM_EPS = 1e-6
NEG_INF = -1e30

N_LAT = BATCH * SEQ
N_CTX = BATCH * CTX_LEN
N_TOK = N_LAT + N_CTX

LANES = 128
TM = 512
TM_PROJ = 1024
ROW_BLK = 256
ITEM_ROWS = 2048
N_ITEMS = N_EXPERTS + (2 * N_TOK) // ITEM_ROWS
FF_TILE = 128
DOWN_TILE = 256
VMEM_LIMIT = 56 * 1024 * 1024

F32 = jnp.float32
BF16 = jnp.bfloat16


def _cparams(sem):
    return pltpu.CompilerParams(dimension_semantics=sem, vmem_limit_bytes=VMEM_LIMIT)


def _row_group(i, n_lat_tiles, tiles_per_batch):
    return jnp.where(i < n_lat_tiles, i // tiles_per_batch, BATCH)


def _ada_kernel(src_ref, w_ref, b_ref, o_ref):
    s = src_ref[...]
    s = (s * jax.nn.sigmoid(s)).astype(BF16)
    acc = jnp.dot(s, w_ref[...].astype(BF16), preferred_element_type=F32)
    o_ref[...] = acc + b_ref[...]


def _ada_all(src8, ada_w, ada_b):
    tn = 1024
    n6 = ada_w.shape[-1]
    return pl.pallas_call(
        _ada_kernel,
        grid=(DEPTH, n6 // tn),
        in_specs=[
            pl.BlockSpec((8, D_MODEL), lambda l, j: (0, 0)),
            pl.BlockSpec((None, D_MODEL, tn), lambda l, j: (l, 0, j)),
            pl.BlockSpec((None, 1, tn), lambda l, j: (l, 0, j)),
        ],
        out_specs=pl.BlockSpec((None, 8, tn), lambda l, j: (l, 0, j)),
        out_shape=jax.ShapeDtypeStruct((DEPTH, 8, n6), F32),
        compiler_params=_cparams(("arbitrary", "arbitrary")),
        name="ada_ln",
    )(src8, ada_w, ada_b.reshape(DEPTH, 1, n6))


def _mod_spec(layer, which, n_lat_tiles, tiles_per_batch, ngrid):
    def imap(*idx):
        return (layer, _row_group(idx[0], n_lat_tiles, tiles_per_batch), which, 0, 0)
    del ngrid
    return pl.BlockSpec((None, None, None, 1, D_MODEL), imap)


def _norm_mod(x, g, sh, sc):
    y = x * lax.rsqrt(jnp.mean(x * x, axis=-1, keepdims=True) + NORM_EPS) * g
    return y * (1.0 + sc) + sh


def _norm1_kernel(x_ref, g_ref, sh_ref, sc_ref, o_ref):
    o_ref[...] = _norm_mod(x_ref[...], g_ref[...], sh_ref[...], sc_ref[...]).astype(o_ref.dtype)


def _norm1(x, norm_g4, mod5, layer):
    n = x.shape[0]
    nlt, tpb = N_LAT // TM, SEQ // TM
    return pl.pallas_call(
        _norm1_kernel,
        grid=(n // TM,),
        in_specs=[
            pl.BlockSpec((TM, D_MODEL), lambda i: (i, 0)),
            pl.BlockSpec((None, None, 1, D_MODEL), lambda i: (layer, 0, 0, 0)),
            _mod_spec(layer, 0, nlt, tpb, 1),
            _mod_spec(layer, 1, nlt, tpb, 1),
        ],
        out_specs=pl.BlockSpec((TM, D_MODEL), lambda i: (i, 0)),
        out_shape=jax.ShapeDtypeStruct((n, D_MODEL), BF16),
        compiler_params=_cparams(("arbitrary",)),
        name="norm1",
    )(x, norm_g4, mod5, mod5)


def _norm2_kernel(x_ref, g_ref, sh_ref, sc_ref, rw_ref, hp_ref, lg_ref):
    h = _norm_mod(x_ref[...], g_ref[...], sh_ref[...], sc_ref[...])
    lg_ref[...] = lax.dot_general(rw_ref[...], h, (((1,), (1,)), ((), ())),
                                  precision=lax.Precision.HIGHEST, preferred_element_type=F32)
    half = D_MODEL // 2
    hp_ref[...] = pltpu.pack_elementwise([h[:, :half], h[:, half:]], packed_dtype=BF16)


def _norm2(x, norm_g4, mod5, router_wt, layer):
    n = x.shape[0]
    nlt, tpb = N_LAT // TM, SEQ // TM
    return pl.pallas_call(
        _norm2_kernel,
        grid=(n // TM,),
        in_specs=[
            pl.BlockSpec((TM, D_MODEL), lambda i: (i, 0)),
            pl.BlockSpec((None, None, 1, D_MODEL), lambda i: (layer, 1, 0, 0)),
            _mod_spec(layer, 3, nlt, tpb, 1),
            _mod_spec(layer, 4, nlt, tpb, 1),
            pl.BlockSpec((N_EXPERTS, D_MODEL), lambda i: (0, 0)),
        ],
        out_specs=[
            pl.BlockSpec((TM, D_MODEL // 2), lambda i: (i, 0)),
            pl.BlockSpec((N_EXPERTS, TM), lambda i: (0, i)),
        ],
        out_shape=[
            jax.ShapeDtypeStruct((n, D_MODEL // 2), jnp.uint32),
            jax.ShapeDtypeStruct((N_EXPERTS, n), F32),
        ],
        compiler_params=_cparams(("arbitrary",)),
        name="norm2_router_logits",
    )(x, norm_g4, mod5, mod5, router_wt)


def _mm_kernel(*refs, nk, n_extra, epilogue):
    a_ref, w_ref = refs[0], refs[1]
    extra = refs[2:2 + n_extra]
    o_ref = refs[2 + n_extra]
    part = jnp.dot(a_ref[...].astype(BF16), w_ref[...].astype(BF16), preferred_element_type=F32)
    if nk == 1:
        epilogue(part, o_ref, *extra)
        return
    acc_ref = refs[3 + n_extra]
    k = pl.program_id(2)

    @pl.when(k == 0)
    def _():
        acc_ref[...] = part

    @pl.when(k > 0)
    def _():
        acc_ref[...] += part

    @pl.when(k == nk - 1)
    def _():
        epilogue(acc_ref[...], o_ref, *extra)


def _matmul(a, a_spec, w, w_spec, extras, out_shape, out_spec, grid, epilogue, tm, tn, name):
    nk = grid[2]
    kern = functools.partial(_mm_kernel, nk=nk, n_extra=len(extras), epilogue=epilogue)
    scratch = [pltpu.VMEM((tm, tn), F32)] if nk > 1 else []
    return pl.pallas_call(
        kern,
        grid=grid,
        in_specs=[a_spec, w_spec] + [s for _, s in extras],
        out_specs=out_spec,
        out_shape=out_shape,
        scratch_shapes=scratch,
        compiler_params=_cparams(("arbitrary", "arbitrary", "arbitrary")),
        name=name,
    )(a, w, *[x for x, _ in extras])


def _epi_plain(acc, o_ref):
    o_ref[...] = acc.astype(o_ref.dtype)


def _epi_resid_gate(acc, o_ref, x_ref, gate_ref):
    o_ref[...] = x_ref[...] + gate_ref[...] * acc


def _epi_rmsnorm(acc, o_ref, g_ref):
    y = acc * lax.rsqrt(jnp.mean(acc * acc, axis=-1, keepdims=True) + NORM_EPS) * g_ref[...]
    o_ref[...] = y.astype(o_ref.dtype)


def _rope_rotate(acc, cos_ref, sin_ref):
    tn = acc.shape[1]
    reps = tn // LANES
    cos = jnp.concatenate([cos_ref[...]] * reps, axis=1) if reps > 1 else cos_ref[...]
    sin = jnp.concatenate([sin_ref[...]] * reps, axis=1) if reps > 1 else sin_ref[...]
    up = pltpu.roll(acc, tn - 16, 1)
    dn = pltpu.roll(acc, 16, 1)
    lane = lax.broadcasted_iota(jnp.int32, acc.shape, 1)
    rot = jnp.where((lane % 32) < 16, up, dn)
    return acc * cos + rot * sin


def _epi_rope(acc, o_ref, cos_ref, sin_ref, *, rope_cols, scale_cols, scale):
    tn = acc.shape[1]
    j = pl.program_id(0)
    touches_rope = (j * tn < rope_cols[1]) & ((j + 1) * tn > rope_cols[0])

    def finish(r):
        if scale_cols:
            col = j * tn + lax.broadcasted_iota(jnp.int32, acc.shape, 1)
            r = jnp.where(col < scale_cols, r * scale, r)
        o_ref[...] = r.astype(o_ref.dtype)

    @pl.when(touches_rope)
    def _():
        col = j * tn + lax.broadcasted_iota(jnp.int32, acc.shape, 1)
        r = _rope_rotate(acc, cos_ref, sin_ref)
        finish(jnp.where((col >= rope_cols[0]) & (col < rope_cols[1]), r, acc))

    @pl.when(jnp.logical_not(touches_rope))
    def _():
        finish(acc)


def _rope_tables(tm):
    d_axis = A_DH // 2
    inv_freq = ROPE_THETA ** (-jnp.arange(0, d_axis, 2, dtype=F32) / d_axis)
    t = jnp.arange(SEQ, dtype=jnp.int32)
    ang_r = (t // GRID_W).astype(F32)[:, None] * inv_freq
    ang_c = (t % GRID_W).astype(F32)[:, None] * inv_freq
    ang = jnp.concatenate([ang_r, ang_r, ang_c, ang_c] * 2, axis=1)
    sign = jnp.tile(jnp.concatenate([-jnp.ones(16, F32), jnp.ones(16, F32)]), 4)
    cos = jnp.concatenate([jnp.cos(ang), jnp.ones((tm, LANES), F32)], axis=0)
    sin = jnp.concatenate([jnp.sin(ang) * sign, jnp.zeros((tm, LANES), F32)], axis=0)
    return cos, sin


def _rope_spec(tm):
    nlt, tpb = N_LAT // tm, SEQ // tm
    return pl.BlockSpec((tm, LANES), lambda j, i, k: (jnp.where(i < nlt, i % tpb, tpb), 0))


def _proj(a, a_col, w, layer, n_out, tn, epilogue, extras, out_dtype, name):
    n = a.shape[0]
    kdim = w.shape[-2]
    grid = (n_out // tn, pl.cdiv(n, TM_PROJ), 1)
    if w.ndim == 3:
        w_spec = pl.BlockSpec((None, kdim, tn), lambda j, i, k: (layer, 0, j))
    else:
        w_spec = pl.BlockSpec((kdim, tn), lambda j, i, k: (0, j))
    return _matmul(
        a, pl.BlockSpec((TM_PROJ, kdim), lambda j, i, k: (i, a_col)), w, w_spec, extras,
        jax.ShapeDtypeStruct((n, n_out), out_dtype), pl.BlockSpec((TM_PROJ, tn), lambda j, i, k: (i, j)),
        grid, epilogue, TM_PROJ, tn, name)


def _out_proj_resid(o, w3, layer_w, x, mod5, layer, name):
    nlt, tpb = N_LAT // TM_PROJ, SEQ // TM_PROJ
    tn = 1024
    gate_spec = pl.BlockSpec(
        (None, None, None, 1, tn),
        lambda j, i, k: (layer, _row_group(i, nlt, tpb), 2, 0, j))
    extras = [(x, pl.BlockSpec((TM_PROJ, tn), lambda j, i, k: (i, j))), (mod5, gate_spec)]
    return _proj(o, 0, w3, layer_w, D_MODEL, tn, _epi_resid_gate, extras, F32, name)


def _dup_half(t, upper):
    h = t[:, A_DH:] if upper else t[:, :A_DH]
    return jnp.concatenate([h, h], axis=1)


def _gqa_kernel(sink_ref, q_ref, kp_ref, ks_ref, kn_ref, vp_ref, vs_ref, vn_ref, kc_ref, vc_ref, o_ref,
                *, n_lat_blocks, blocks_per_seq):
    n = pl.program_id(0)
    is_lat = n < n_lat_blocks
    nb = n % blocks_per_seq
    iq = lax.broadcasted_iota(jnp.int32, (BLK, 3 * BLK), 0)
    ik = lax.broadcasted_iota(jnp.int32, (BLK, 3 * BLK), 1)
    mask = (ik >= iq) & (ik <= iq + 2 * BLK)
    mask = mask & ((nb > 0) | (ik >= BLK)) & ((nb < blocks_per_seq - 1) | (ik < 2 * BLK)) & is_lat
    lane = lax.broadcasted_iota(jnp.int32, (BLK, LANES), 1)
    lo_half = lane < A_DH
    keep_lo = lo_half.astype(F32).astype(BF16)
    keep_hi = (1.0 - lo_half.astype(F32)).astype(BF16)
    n_pairs = A_GROUPS // 2
    for hk in range(A_HKV):
        c0 = (hk // 2) * LANES
        upper = (hk % 2) == 1
        k_all = jnp.concatenate([
            _dup_half(kp_ref[:, c0:c0 + LANES], upper), _dup_half(ks_ref[:, c0:c0 + LANES], upper),
            _dup_half(kn_ref[:, c0:c0 + LANES], upper), _dup_half(kc_ref[:, c0:c0 + LANES], upper)], axis=0)
        v_all = jnp.concatenate([
            _dup_half(vp_ref[:, c0:c0 + LANES], upper), _dup_half(vs_ref[:, c0:c0 + LANES], upper),
            _dup_half(vn_ref[:, c0:c0 + LANES], upper), _dup_half(vc_ref[:, c0:c0 + LANES], upper)], axis=0)
        ones = jnp.ones(v_all.shape, BF16)
        v_lo = v_all * keep_lo[0:1] + ones * keep_hi[0:1]
        v_hi = v_all * keep_hi[0:1] + ones * keep_lo[0:1]
        lhs = []
        for p in range(n_pairs):
            qp = q_ref[:, hk * A_GROUPS * A_DH + p * LANES: hk * A_GROUPS * A_DH + (p + 1) * LANES]
            lhs.append(qp * keep_lo)
            lhs.append(qp * keep_hi)
        lhs = jnp.concatenate(lhs, axis=0)
        s = lax.dot_general(lhs, k_all, (((1,), (1,)), ((), ())), preferred_element_type=F32)
        probs, sink_terms = [], []
        for g in range(A_GROUPS):
            sg = s[g * BLK:(g + 1) * BLK]
            s_loc = jnp.where(mask, sg[:, :3 * BLK], NEG_INF)
            s_ctx = sg[:, 3 * BLK:]
            sink = sink_ref[hk * A_GROUPS + g]
            m = jnp.maximum(jnp.maximum(jnp.max(s_loc, axis=-1, keepdims=True),
                                        jnp.max(s_ctx, axis=-1, keepdims=True)), sink)
            sink_terms.append(jnp.exp(sink - m))
            probs.append(jnp.exp(jnp.concatenate([s_loc - m, s_ctx - m], axis=1).astype(BF16)))
        res_lo = jnp.dot(jnp.concatenate(probs[0::2], axis=0), v_lo, preferred_element_type=F32)
        res_hi = jnp.dot(jnp.concatenate(probs[1::2], axis=0), v_hi, preferred_element_type=F32)
        for p in range(n_pairs):
            r_lo = res_lo[p * BLK:(p + 1) * BLK]
            r_hi = res_hi[p * BLK:(p + 1) * BLK]
            o_lo = r_lo / (r_lo[:, A_DH:A_DH + 1] + sink_terms[2 * p])
            o_hi = r_hi / (r_hi[:, 0:1] + sink_terms[2 * p + 1])
            col = hk * A_GROUPS * A_DH + p * LANES
            o_ref[:, col:col + LANES] = jnp.where(lo_half, o_lo, o_hi).astype(o_ref.dtype)


def _gqa_attention(qkv, sink, need_ctx):
    bps = SEQ // BLK
    nlb = BATCH * bps
    n_ctx_blocks = N_CTX // BLK if need_ctx else 0
    nq = nlb + n_ctx_blocks
    kcol, vcol = (A_HQ * A_DH) // 256, (A_HQ * A_DH) // 256 + 1
    cpb = CTX_LEN // BLK

    def batch_of(n):
        return jnp.where(n < nlb, n // bps, (n - nlb) // cpb)

    def loc_map(delta, col):
        def imap(n, s):
            nb = n % bps
            blk = jnp.clip(nb + delta, 0, bps - 1)
            return (jnp.where(n < nlb, (n // bps) * bps + blk, 0), col)
        return imap

    def ctx_map(col):
        return lambda n, s: (N_LAT // CTX_LEN + batch_of(n), col)

    kv_blk = (BLK, 256)
    grid_spec = pltpu.PrefetchScalarGridSpec(
        num_scalar_prefetch=1,
        grid=(nq,),
        in_specs=[
            pl.BlockSpec((BLK, A_HQ * A_DH), lambda n, s: (n, 0)),
            pl.BlockSpec(kv_blk, loc_map(-1, kcol)), pl.BlockSpec(kv_blk, loc_map(0, kcol)),
            pl.BlockSpec(kv_blk, loc_map(1, kcol)),
            pl.BlockSpec(kv_blk, loc_map(-1, vcol)), pl.BlockSpec(kv_blk, loc_map(0, vcol)),
            pl.BlockSpec(kv_blk, loc_map(1, vcol)),
            pl.BlockSpec((CTX_LEN, 256), ctx_map(kcol)), pl.BlockSpec((CTX_LEN, 256), ctx_map(vcol)),
        ],
        out_specs=pl.BlockSpec((BLK, A_HQ * A_DH), lambda n, s: (n, 0)),
    )
    kern = functools.partial(_gqa_kernel, n_lat_blocks=nlb, blocks_per_seq=bps)
    return pl.pallas_call(
        kern,
        grid_spec=grid_spec,
        out_shape=jax.ShapeDtypeStruct((nq * BLK, A_HQ * A_DH), BF16),
        compiler_params=_cparams(("arbitrary",)),
        name="gqa_window_attn",
    )(sink, qkv, qkv, qkv, qkv, qkv, qkv, qkv, qkv, qkv)


MLA_TQ = 1024
MLA_TK = 512


def _mla_kernel(*refs, n_lat_keys):
    if n_lat_keys:
        (qn_ref, qr_ref, kn_ref, v_ref, kr_ref, knc_ref, vc_ref, krc_ref, o_ref, kcat, vcat) = refs
    else:
        (qn_ref, qr_ref, knc_ref, vc_ref, krc_ref, o_ref, kcat, vcat) = refs
    qi = pl.program_id(2)
    nk = n_lat_keys + CTX_LEN

    @pl.when(qi == 0)
    def _():
        if n_lat_keys:
            kcat[0:n_lat_keys, 0:LANES] = kn_ref[...]
            kcat[0:n_lat_keys, LANES:2 * LANES] = kr_ref[...]
            vcat[0:n_lat_keys, 0:M_V] = v_ref[...]
        kcat[n_lat_keys:nk, 0:LANES] = knc_ref[...]
        kcat[n_lat_keys:nk, LANES:2 * LANES] = krc_ref[...]
        vcat[n_lat_keys:nk, 0:M_V] = vc_ref[...]
        vcat[:, M_V:M_V + LANES] = jnp.ones((nk, LANES), BF16)

    q = jnp.concatenate([qn_ref[...], qr_ref[...]], axis=1)

    def chunk(k0, size):
        kc = kcat[k0:k0 + size, :]
        vc = vcat[k0:k0 + size, :]
        s = lax.dot_general(q, kc, (((1,), (1,)), ((), ())), preferred_element_type=F32)
        m_c = jnp.max(s, axis=-1, keepdims=True)
        p = jnp.exp((s - m_c).astype(BF16))
        ol = jnp.dot(p, vc, preferred_element_type=F32)
        return m_c, ol[:, M_V:M_V + 1], ol[:, :M_V]

    parts = [chunk(c * MLA_TK, MLA_TK) for c in range(n_lat_keys // MLA_TK)]
    parts.append(chunk(n_lat_keys, CTX_LEN))
    m = parts[0][0]
    for m_c, _, _ in parts[1:]:
        m = jnp.maximum(m, m_c)
    l = None
    acc = None
    for m_c, l_c, o_c in parts:
        w_c = jnp.exp(m_c - m)
        l = w_c * l_c if l is None else l + w_c * l_c
        acc = w_c * o_c if acc is None else acc + w_c * o_c
    o_ref[...] = (acc / l).astype(o_ref.dtype)


def _mla_attention(qcat, kv, kr, latent):
    hq = M_H
    ctx_row_blk = N_LAT // CTX_LEN
    if latent:
        tq, nqt, n_lat_keys = MLA_TQ, SEQ // MLA_TQ, SEQ
        row0 = lambda b, qi: b * nqt + qi
        n_out = N_LAT
    else:
        tq, nqt, n_lat_keys = CTX_LEN, 1, 0
        row0 = lambda b, qi: ctx_row_blk + b
        n_out = N_CTX
    in_specs = [
        pl.BlockSpec((tq, LANES), lambda b, h, qi: (row0(b, qi), h)),
        pl.BlockSpec((tq, LANES), lambda b, h, qi: (row0(b, qi), hq + h)),
    ]
    args = [qcat, qcat]
    if latent:
        in_specs += [
            pl.BlockSpec((SEQ, LANES), lambda b, h, qi: (b, 2 * h)),
            pl.BlockSpec((SEQ, LANES), lambda b, h, qi: (b, 2 * h + 1)),
            pl.BlockSpec((SEQ, LANES), lambda b, h, qi: (b, 0)),
        ]
        args += [kv, kv, kr]
    in_specs += [
        pl.BlockSpec((CTX_LEN, LANES), lambda b, h, qi: (ctx_row_blk + b, 2 * h)),
        pl.BlockSpec((CTX_LEN, LANES), lambda b, h, qi: (ctx_row_blk + b, 2 * h + 1)),
        pl.BlockSpec((CTX_LEN, LANES), lambda b, h, qi: (ctx_row_blk + b, 0)),
    ]
    args += [kv, kv, kr]
    if latent:
        out_spec = pl.BlockSpec((tq, LANES), lambda b, h, qi: (b * nqt + qi, h))
    else:
        out_spec = pl.BlockSpec((tq, LANES), lambda b, h, qi: (b, h))
    nk = n_lat_keys + CTX_LEN
    return pl.pallas_call(
        functools.partial(_mla_kernel, n_lat_keys=n_lat_keys),
        grid=(BATCH, hq, nqt),
        in_specs=in_specs,
        out_specs=out_spec,
        out_shape=jax.ShapeDtypeStruct((n_out, hq * M_V), BF16),
        scratch_shapes=[pltpu.VMEM((nk, 2 * LANES), BF16), pltpu.VMEM((nk, M_V + LANES), BF16)],
        compiler_params=_cparams(("arbitrary", "arbitrary", "arbitrary")),
        name="mla_attn_lat" if latent else "mla_attn_ctx",
    )(*args)


def _dft_mats(n, scale):
    sub = 64
    k = jnp.arange(n, dtype=jnp.int32)
    j1 = jnp.arange(n // sub, dtype=jnp.int32) * sub
    j0 = jnp.arange(sub, dtype=jnp.int32)
    a = ((j1[:, None] * k[None, :]) % n).astype(F32) * (2.0 * math.pi / n)
    b = ((j0[:, None] * k[None, :]) % n).astype(F32) * (2.0 * math.pi / n)
    ca, sa = (jnp.cos(a) * scale)[:, None, :], (jnp.sin(a) * scale)[:, None, :]
    cb, sb = jnp.cos(b)[None, :, :], jnp.sin(b)[None, :, :]
    return (ca * cb - sa * sb).reshape(n, n), (sa * cb + ca * sb).reshape(n, n)


def _fnet_stage1_kernel(h_ref, w_ref, z_ref):
    z = jnp.dot(h_ref[...], w_ref[...], preferred_element_type=F32)
    z_ref[0] = z[:, :F_GROUP_DIM].astype(z_ref.dtype)
    z_ref[1] = z[:, F_GROUP_DIM:].astype(z_ref.dtype)


def _fnet_stage1(h1, w1, row_blk0, seq_len, nbatch):
    tm = min(TM, seq_len)
    tiles = seq_len // tm
    return pl.pallas_call(
        _fnet_stage1_kernel,
        grid=(nbatch, tiles, F_GROUPS),
        in_specs=[
            pl.BlockSpec((tm, F_GROUP_DIM), lambda b, i, g: (row_blk0 + b * tiles + i, g)),
            pl.BlockSpec((F_GROUP_DIM, 2 * F_GROUP_DIM), lambda b, i, g: (0, 0)),
        ],
        out_specs=pl.BlockSpec((None, 2, tm, F_GROUP_DIM), lambda b, i, g: (b, 0, i, g)),
        out_shape=jax.ShapeDtypeStruct((nbatch, 2, seq_len, D_MODEL), BF16),
        compiler_params=_cparams(("arbitrary", "arbitrary", "arbitrary")),
        name="fnet_channel_dft",
    )(h1, w1)


def _fnet_stage2_kernel(a_ref, z_ref, o_ref, acc_ref, *, nk):
    k = pl.program_id(2)
    part = jnp.dot(a_ref[...], z_ref[...], preferred_element_type=F32)

    @pl.when(k == 0)
    def _():
        acc_ref[...] = part

    @pl.when(k > 0)
    def _():
        acc_ref[...] += part

    @pl.when(k == nk - 1)
    def _():
        o_ref[...] = acc_ref[...].astype(o_ref.dtype)


def _fnet_stage2(a2, z, seq_len, nbatch):
    tm = min(1024, seq_len)
    tk = min(512, seq_len)
    nk = 2 * seq_len // tk
    z2 = z.reshape(nbatch, 2 * seq_len, D_MODEL)
    return pl.pallas_call(
        functools.partial(_fnet_stage2_kernel, nk=nk),
        grid=(nbatch, seq_len // tm, nk),
        in_specs=[
            pl.BlockSpec((tm, tk), lambda b, i, k: (i, k)),
            pl.BlockSpec((None, tk, D_MODEL), lambda b, i, k: (b, k, 0)),
        ],
        out_specs=pl.BlockSpec((None, tm, D_MODEL), lambda b, i, k: (b, i, 0)),
        out_shape=jax.ShapeDtypeStruct((nbatch, seq_len, D_MODEL), BF16),
        scratch_shapes=[pltpu.VMEM((tm, D_MODEL), F32)],
        compiler_params=_cparams(("arbitrary", "arbitrary", "arbitrary")),
        name="fnet_position_dft",
    )(a2, z2)


ROUTER_TILE = 512


def _top2_rows(v, rowf):
    big = float(v.shape[0])
    m1 = jnp.max(v, axis=0, keepdims=True)
    i1 = jnp.min(jnp.where(v == m1, rowf, big), axis=0, keepdims=True)
    v2 = jnp.where(rowf == i1, NEG_INF, v)
    m2 = jnp.max(v2, axis=0, keepdims=True)
    i2 = jnp.min(jnp.where(v2 == m2, rowf, big), axis=0, keepdims=True)
    return m1, i1, m2, i2


def _router_kernel(lg_ref, b_ref, idx_ref, w_ref, cnt_ref, carry_ref):
    step = pl.program_id(0)
    t = lg_ref.shape[1]

    @pl.when(step == 0)
    def _():
        carry_ref[...] = jnp.zeros_like(carry_ref)

    scores = jax.nn.sigmoid(lg_ref[...])
    sel = scores + b_ref[...][:, 0:1]
    row = lax.broadcasted_iota(jnp.int32, (N_EXPERTS, t), 0)
    rowf = row.astype(F32)
    grp = row // EXPERTS_PER_GROUP
    best = jnp.zeros((1, t), jnp.int32)
    best_v = None
    for g in range(N_EXPERT_GROUPS):
        m1, _, m2, _ = _top2_rows(jnp.where(grp == g, sel, NEG_INF), rowf)
        gs = m1 + m2
        if g == 0:
            best_v = gs
        else:
            upd = gs > best_v
            best = jnp.where(upd, g, best)
            best_v = jnp.where(upd, gs, best_v)
    _, e0, _, e1 = _top2_rows(jnp.where(grp == best, sel, NEG_INF), rowf)
    oh0 = rowf == e0
    oh1 = rowf == e1
    s0 = jnp.sum(jnp.where(oh0, scores, 0.0), axis=0, keepdims=True)
    s1 = jnp.sum(jnp.where(oh1, scores, 0.0), axis=0, keepdims=True)
    tot = s0 + s1
    cnt = jnp.where(oh0 | oh1, 1.0, 0.0)
    a = lax.broadcasted_iota(jnp.int32, (t, t), 0)
    bcol = lax.broadcasted_iota(jnp.int32, (t, t), 1)
    tri = jnp.where(a < bcol, 1.0, 0.0).astype(BF16)
    excl = jnp.dot(cnt.astype(BF16), tri, preferred_element_type=F32) + carry_ref[...][:, 0:1]
    r0 = jnp.sum(jnp.where(oh0, excl, 0.0), axis=0, keepdims=True)
    r1 = jnp.sum(jnp.where(oh1, excl, 0.0), axis=0, keepdims=True)
    new_carry = carry_ref[...] + jnp.sum(cnt, axis=1, keepdims=True)
    carry_ref[...] = new_carry
    idx_ref[...] = jnp.zeros(idx_ref.shape, jnp.int32)
    idx_ref[0:1, :] = e0.astype(jnp.int32)
    idx_ref[1:2, :] = e1.astype(jnp.int32)
    idx_ref[2:3, :] = r0.astype(jnp.int32)
    idx_ref[3:4, :] = r1.astype(jnp.int32)
    w_ref[...] = jnp.zeros(w_ref.shape, F32)
    w_ref[0:1, :] = s0 / tot
    w_ref[1:2, :] = s1 / tot
    cnt_ref[...] = new_carry.astype(jnp.int32)


def _router(logits_t, router_b):
    n = logits_t.shape[1]
    t = ROUTER_TILE
    b2 = jnp.broadcast_to(router_b.astype(F32)[:, None], (N_EXPERTS, LANES))
    return pl.pallas_call(
        _router_kernel,
        grid=(n // t,),
        in_specs=[pl.BlockSpec((N_EXPERTS, t), lambda i: (0, i)),
                  pl.BlockSpec((N_EXPERTS, LANES), lambda i: (0, 0))],
        out_specs=[pl.BlockSpec((8, t), lambda i: (0, i)),
                   pl.BlockSpec((8, t), lambda i: (0, i)),
                   pl.BlockSpec((N_EXPERTS, LANES), lambda i: (0, 0))],
        out_shape=[jax.ShapeDtypeStruct((8, n), jnp.int32),
                   jax.ShapeDtypeStruct((8, n), F32),
                   jax.ShapeDtypeStruct((N_EXPERTS, LANES), jnp.int32)],
        scratch_shapes=[pltpu.VMEM((N_EXPERTS, LANES), F32)],
        compiler_params=_cparams(("arbitrary",)),
        name="moe_router",
    )(logits_t, b2)


def _plan_items(counts):
    n_chunks = (counts + ITEM_ROWS - 1) // ITEM_ROWS
    ends = jnp.cumsum(n_chunks)
    starts = ends - n_chunks
    base = starts * ITEM_ROWS
    n_valid = ends[-1]
    item = jnp.arange(N_ITEMS, dtype=jnp.int32)
    item_src = jnp.minimum(item, n_valid - 1)
    expert = jnp.sum((ends[None, :] <= item_src[:, None]).astype(jnp.int32), axis=1)
    rows = jnp.clip(counts[expert] - (item_src - starts[expert]) * ITEM_ROWS, 0, ITEM_ROWS)
    rows = jnp.where(item < n_valid, rows, 0).astype(jnp.int32)
    nblk = (rows + ROW_BLK - 1) // ROW_BLK
    return base.astype(jnp.int32), item_src.astype(jnp.int32), expert, nblk, rows


DISPATCH_TILE = 256


def _dispatch_kernel(p0_ref, p1_ref, hp_ref, hs_ref, sem):
    base = pl.program_id(0) * DISPATCH_TILE

    def copies(r):
        t = base + r
        return (pltpu.make_async_copy(hp_ref.at[pl.ds(r, 1)], hs_ref.at[pl.ds(p0_ref[t], 1)], sem),
                pltpu.make_async_copy(hp_ref.at[pl.ds(r, 1)], hs_ref.at[pl.ds(p1_ref[t], 1)], sem))

    def start(r, c):
        a, b = copies(r)
        a.start(priority=0)
        b.start(priority=1)
        return c

    def wait(r, c):
        a, b = copies(r)
        a.wait()
        b.wait()
        return c

    lax.fori_loop(0, DISPATCH_TILE, start, 0, unroll=8)
    lax.fori_loop(0, DISPATCH_TILE, wait, 0, unroll=8)


def _dispatch(hp, pos0, pos1):
    n = hp.shape[0]
    grid_spec = pltpu.PrefetchScalarGridSpec(
        num_scalar_prefetch=2,
        grid=(n // DISPATCH_TILE,),
        in_specs=[pl.BlockSpec((DISPATCH_TILE, D_MODEL // 2), lambda i, p0, p1: (i, 0))],
        out_specs=pl.BlockSpec(memory_space=pl.ANY),
        scratch_shapes=[pltpu.SemaphoreType.DMA(())],
    )
    return pl.pallas_call(
        _dispatch_kernel,
        grid_spec=grid_spec,
        out_shape=jax.ShapeDtypeStruct((N_ITEMS * ITEM_ROWS, D_MODEL // 2), hp.dtype),
        compiler_params=pltpu.CompilerParams(dimension_semantics=("arbitrary",), has_side_effects=True),
        name="moe_dispatch",
    )(pos0, pos1, hp)


def _gate_up_kernel(src_ref, exp_ref, nblk_ref, rows_ref, hs_ref, wg_ref, wu_ref, a_ref, hb_ref):
    del src_ref, exp_ref
    i = pl.program_id(0)
    f = pl.program_id(1)
    nblk = nblk_ref[i]
    n_rows = rows_ref[i]
    half = D_MODEL // 2

    @pl.when(nblk > 0)
    def _():
        @pl.when(f == 0)
        def _():
            def unpack(r, c):
                rows = pl.ds(pl.multiple_of(r * ROW_BLK, ROW_BLK), ROW_BLK)
                u = hs_ref[rows, :]
                live = (r * ROW_BLK + lax.broadcasted_iota(jnp.int32, (ROW_BLK, half), 0)) < n_rows
                lo = pltpu.unpack_elementwise(u, index=0, packed_dtype=BF16, unpacked_dtype=F32)
                hi = pltpu.unpack_elementwise(u, index=1, packed_dtype=BF16, unpacked_dtype=F32)
                hb_ref[rows, 0:half] = jnp.where(live, lo, 0.0).astype(BF16)
                hb_ref[rows, half:D_MODEL] = jnp.where(live, hi, 0.0).astype(BF16)
                return c
            lax.fori_loop(0, nblk, unpack, 0)

        for m in range(1, ITEM_ROWS // ROW_BLK + 1):
            rows = m * ROW_BLK

            @pl.when(nblk == m)
            def _():
                w = jnp.concatenate([wg_ref[...].astype(BF16), wu_ref[...].astype(BF16)], axis=1)
                res = jnp.dot(hb_ref[0:rows, :], w, preferred_element_type=F32)
                g = res[:, :FF_TILE]
                u = res[:, FF_TILE:]
                a_ref[0:rows, :] = (g * jax.nn.sigmoid(g) * u).astype(a_ref.dtype)
                if rows < ITEM_ROWS:
                    a_ref[rows:ITEM_ROWS, :] = jnp.zeros((ITEM_ROWS - rows, FF_TILE), a_ref.dtype)


def _gate_up(hs, w_gate, w_up, layer, item_src, expert, nblk, rows):
    nf = D_FF // FF_TILE

    def frozen_f(i, f, nb):
        return jnp.where(nb[i] > 0, f, nf - 1)

    grid_spec = pltpu.PrefetchScalarGridSpec(
        num_scalar_prefetch=4,
        grid=(N_ITEMS, nf),
        in_specs=[
            pl.BlockSpec((ITEM_ROWS, D_MODEL // 2), lambda i, f, src, ex, nb, rw: (src[i], 0)),
            pl.BlockSpec((None, None, D_MODEL, FF_TILE),
                         lambda i, f, src, ex, nb, rw: (layer, ex[i], 0, frozen_f(i, f, nb))),
            pl.BlockSpec((None, None, D_MODEL, FF_TILE),
                         lambda i, f, src, ex, nb, rw: (layer, ex[i], 0, frozen_f(i, f, nb))),
        ],
        out_specs=pl.BlockSpec((ITEM_ROWS, FF_TILE),
                               lambda i, f, src, ex, nb, rw: (src[i], frozen_f(i, f, nb))),
        scratch_shapes=[pltpu.VMEM((ITEM_ROWS, D_MODEL), BF16)],
    )
    return pl.pallas_call(
        _gate_up_kernel,
        grid_spec=grid_spec,
        out_shape=jax.ShapeDtypeStruct((N_ITEMS * ITEM_ROWS, D_FF), BF16),
        compiler_params=_cparams(("arbitrary", "arbitrary")),
        name="moe_gate_up",
    )(item_src, expert, nblk, rows, hs, w_gate, w_up)


def _down_kernel(src_ref, exp_ref, nblk_ref, a_ref, wlo_ref, whi_ref, y_ref):
    del src_ref, exp_ref
    i = pl.program_id(0)
    nblk = nblk_ref[i]

    for m in range(1, ITEM_ROWS // ROW_BLK + 1):
        rows = m * ROW_BLK

        @pl.when(nblk == m)
        def _():
            w = jnp.concatenate([wlo_ref[...].astype(BF16), whi_ref[...].astype(BF16)], axis=1)
            res = jnp.dot(a_ref[0:rows, :], w, preferred_element_type=F32)
            y_ref[0:rows, :] = pltpu.pack_elementwise([res[:, :DOWN_TILE], res[:, DOWN_TILE:]], packed_dtype=BF16)
            if rows < ITEM_ROWS:
                zero = jnp.zeros((ITEM_ROWS - rows, DOWN_TILE), F32)
                y_ref[rows:ITEM_ROWS, :] = pltpu.pack_elementwise([zero, zero], packed_dtype=BF16)


def _down(a, w_down, layer, item_src, expert, nblk):
    nj = D_MODEL // (2 * DOWN_TILE)

    def frozen_j(i, j, nb):
        return jnp.where(nb[i] > 0, j, nj - 1)

    grid_spec = pltpu.PrefetchScalarGridSpec(
        num_scalar_prefetch=3,
        grid=(N_ITEMS, nj),
        in_specs=[
            pl.BlockSpec((ITEM_ROWS, D_FF), lambda i, j, src, ex, nb: (src[i], 0)),
            pl.BlockSpec((None, None, D_FF, DOWN_TILE),
                         lambda i, j, src, ex, nb: (layer, ex[i], 0, frozen_j(i, j, nb))),
            pl.BlockSpec((None, None, D_FF, DOWN_TILE),
                         lambda i, j, src, ex, nb: (layer, ex[i], 0, nj + frozen_j(i, j, nb))),
        ],
        out_specs=pl.BlockSpec((ITEM_ROWS, DOWN_TILE), lambda i, j, src, ex, nb: (src[i], frozen_j(i, j, nb))),
    )
    return pl.pallas_call(
        _down_kernel,
        grid_spec=grid_spec,
        out_shape=jax.ShapeDtypeStruct((N_ITEMS * ITEM_ROWS, D_MODEL // 2), jnp.uint32),
        compiler_params=_cparams(("arbitrary", "arbitrary")),
        name="moe_down",
    )(item_src, expert, nblk, a, w_down, w_down)


COMBINE_TILE = 256


def _combine_kernel(p0_ref, p1_ref, x_ref, gate_ref, w0_ref, w1_ref, g_ref, sh_ref, sc_ref, ys_ref, *rest, final):
    if final:
        o_ref, y0_ref, y1_ref, sem = rest
    else:
        xo_ref, h_ref, y0_ref, y1_ref, sem = rest
    base = pl.program_id(0) * COMBINE_TILE

    def copies(r):
        t = base + r
        return (pltpu.make_async_copy(ys_ref.at[pl.ds(p0_ref[t], 1)], y0_ref.at[pl.ds(r, 1)], sem),
                pltpu.make_async_copy(ys_ref.at[pl.ds(p1_ref[t], 1)], y1_ref.at[pl.ds(r, 1)], sem))

    def start(r, c):
        a, b = copies(r)
        a.start(priority=0)
        b.start(priority=1)
        return c

    def wait(r, c):
        a, b = copies(r)
        a.wait()
        b.wait()
        return c

    lax.fori_loop(0, COMBINE_TILE, start, 0, unroll=8)
    lax.fori_loop(0, COMBINE_TILE, wait, 0, unroll=8)
    half = D_MODEL // 2
    w0 = w0_ref[...]
    w1 = w1_ref[...]
    xs = []
    for hh in range(2):
        cols = slice(hh * half, (hh + 1) * half)
        y = (w0 * pltpu.unpack_elementwise(y0_ref[...], index=hh, packed_dtype=BF16, unpacked_dtype=F32)
             + w1 * pltpu.unpack_elementwise(y1_ref[...], index=hh, packed_dtype=BF16, unpacked_dtype=F32))
        xs.append(x_ref[:, cols] + gate_ref[:, cols] * y)
    ssq = jnp.sum(xs[0] * xs[0], axis=-1, keepdims=True) + jnp.sum(xs[1] * xs[1], axis=-1, keepdims=True)
    inv = lax.rsqrt(ssq * (1.0 / D_MODEL) + NORM_EPS)
    for hh in range(2):
        cols = slice(hh * half, (hh + 1) * half)
        normed = xs[hh] * inv * g_ref[:, cols]
        if final:
            o_ref[:, cols] = normed
        else:
            xo_ref[:, cols] = xs[hh]
            h_ref[:, cols] = (normed * (1.0 + sc_ref[:, cols]) + sh_ref[:, cols]).astype(h_ref.dtype)


def _combine(x, ys, pos0, pos1, w0, w1, mod5, layer, norm_g, next_layer):
    n = x.shape[0]
    t = COMBINE_TILE
    nlt, tpb = N_LAT // t, SEQ // t
    final = next_layer is None
    mod_layer = layer if final else next_layer
    g_layer = 0 if final else next_layer

    def mod_spec(lyr, which):
        return pl.BlockSpec((None, None, None, 1, D_MODEL),
                            lambda i, p0, p1: (lyr, _row_group(i, nlt, tpb), which, 0, 0))

    row_spec = pl.BlockSpec((t, D_MODEL), lambda i, p0, p1: (i, 0))
    grid_spec = pltpu.PrefetchScalarGridSpec(
        num_scalar_prefetch=2,
        grid=(n // t,),
        in_specs=[
            row_spec,
            mod_spec(layer, 5),
            pl.BlockSpec((t, 1), lambda i, p0, p1: (i, 0)),
            pl.BlockSpec((t, 1), lambda i, p0, p1: (i, 0)),
            pl.BlockSpec((None, None, 1, D_MODEL), lambda i, p0, p1: (g_layer, 0, 0, 0)),
            mod_spec(mod_layer, 0),
            mod_spec(mod_layer, 1),
            pl.BlockSpec(memory_space=pl.ANY),
        ],
        out_specs=row_spec if final else [row_spec, row_spec],
        scratch_shapes=[pltpu.VMEM((t, D_MODEL // 2), jnp.uint32), pltpu.VMEM((t, D_MODEL // 2), jnp.uint32),
                        pltpu.SemaphoreType.DMA(())],
    )
    if final:
        out_shape = jax.ShapeDtypeStruct((n, D_MODEL), F32)
    else:
        out_shape = [jax.ShapeDtypeStruct((n, D_MODEL), F32), jax.ShapeDtypeStruct((n, D_MODEL), BF16)]
    return pl.pallas_call(
        functools.partial(_combine_kernel, final=final),
        grid_spec=grid_spec,
        out_shape=out_shape,
        compiler_params=_cparams(("arbitrary",)),
        name="moe_combine_final" if final else "moe_combine",
    )(pos0, pos1, x, mod5, w0, w1, norm_g, mod5, mod5, ys)


def _moe_layer(x, norm_g4, mod5, router_wt, router_b, w_gate, w_up, w_down, layer, out_norm_g, next_layer):
    hp, logits_t = _norm2(x, norm_g4, mod5, router_wt, layer)
    idx, wts, counts = _router(logits_t, router_b)
    base, item_src, expert, nblk, rows = _plan_items(counts[:, 0])
    pos0 = base[idx[0]] + idx[2]
    pos1 = base[idx[1]] + idx[3]
    hs = _dispatch(hp, pos0, pos1)
    a = _gate_up(hs, w_gate, w_up, layer, item_src, expert, nblk, rows)
    ys = _down(a, w_down, layer, item_src, expert, nblk)
    return _combine(x, ys, pos0, pos1, wts[0][:, None], wts[1][:, None], mod5, layer, out_norm_g, next_layer)


def _attn_mixer(h1, x, mod5, layer, j, attn_w_qkv, attn_sink, attn_w_o, rope_cos, rope_sin, need_ctx):
    q_cols = A_HQ * A_DH
    epi = functools.partial(_epi_rope, rope_cols=(0, q_cols + A_HKV * A_DH), scale_cols=q_cols, scale=A_SCALE)
    extras = [(rope_cos, _rope_spec(TM_PROJ)), (rope_sin, _rope_spec(TM_PROJ))]
    qkv = _proj(h1, 0, attn_w_qkv, j, q_cols + 2 * A_HKV * A_DH, 512, epi, extras, BF16, "attn_qkv_proj")
    o = _gqa_attention(qkv, attn_sink[j].astype(F32), need_ctx)
    return _out_proj_resid(o, attn_w_o, j, x, mod5, layer, "attn_out_proj")


def _fnet_mixer(h1, x, mod5, layer, j, fnet_w_o, need_ctx):
    cc, sc = _dft_mats(F_GROUP_DIM, F_GROUP_DIM ** -0.5)
    w1 = jnp.concatenate([cc, sc], axis=1).astype(BF16)
    outs = []
    for seq_len, row_blk0, active in ((SEQ, 0, True), (CTX_LEN, N_LAT // min(TM, CTX_LEN), need_ctx)):
        if not active:
            continue
        cl, sl = _dft_mats(seq_len, seq_len ** -0.5)
        a2 = jnp.concatenate([cl, -sl], axis=1).astype(BF16)
        z = _fnet_stage1(h1, w1, row_blk0, seq_len, BATCH)
        y = _fnet_stage2(a2, z, seq_len, BATCH)
        outs.append(y.reshape(BATCH * seq_len, D_MODEL))
    y_all = jnp.concatenate(outs, axis=0) if len(outs) > 1 else outs[0]
    return _out_proj_resid(y_all, fnet_w_o, j, x, mod5, layer, "fnet_out_proj")


def _mla_mixer(h1, x, mod5, layer, j, mla_w_in, mla_g_q, mla_g_kv, mla_w_uq, mla_w_ukv, mla_w_o,
               rope_cos, rope_sin, need_ctx):
    g_cat = jnp.concatenate([mla_g_q[j], mla_g_kv[j]]).reshape(1, M_QR + M_KVR)
    extras = [(g_cat, pl.BlockSpec((1, M_QR), lambda jj, i, k: (0, jj)))]
    cqkv = _proj(h1, 0, mla_w_in, j, M_QR + M_KVR, M_QR, _epi_rmsnorm, extras, BF16, "mla_in_proj")
    w_kr = jnp.pad(mla_w_in[j][:, M_QR + M_KVR:], ((0, 0), (0, LANES - M_ROPE)))
    epi_kr = functools.partial(_epi_rope, rope_cols=(0, LANES), scale_cols=0, scale=1.0)
    rope_extras = [(rope_cos, _rope_spec(TM_PROJ)), (rope_sin, _rope_spec(TM_PROJ))]
    kr = _proj(h1, 0, w_kr, 0, LANES, LANES, epi_kr, rope_extras, BF16, "mla_kr_proj")
    w_uq = mla_w_uq[j].reshape(M_QR, M_H, M_NOPE + M_ROPE)
    w_qn = w_uq[:, :, :M_NOPE].reshape(M_QR, M_H * M_NOPE)
    w_qr = jnp.pad(w_uq[:, :, M_NOPE:], ((0, 0), (0, 0), (0, LANES - M_ROPE))).reshape(M_QR, M_H * LANES)
    w_q = jnp.concatenate([w_qn, w_qr], axis=1)
    nope_cols = M_H * M_NOPE
    epi_q = functools.partial(_epi_rope, rope_cols=(nope_cols, nope_cols + M_H * LANES),
                              scale_cols=nope_cols + M_H * LANES, scale=M_SCALE)
    qcat = _proj(cqkv, 0, w_q, 0, nope_cols + M_H * LANES, 1024, epi_q, rope_extras, BF16, "mla_q_proj")
    kv = _proj(cqkv, 1, mla_w_ukv, j, M_H * (M_NOPE + M_V), 1024, _epi_plain, [], BF16, "mla_kv_proj")
    o = _mla_attention(qcat, kv, kr, latent=True)
    if need_ctx:
        o = jnp.concatenate([o, _mla_attention(qcat, kv, kr, latent=False)], axis=0)
    return _out_proj_resid(o, mla_w_o, j, x, mod5, layer, "mla_out_proj")


def kernel(x, c, ctx, c_ctx, ada_w, ada_b, norm_g, final_g, attn_w_qkv, attn_sink, attn_w_o,
           fnet_w_o, mla_w_in, mla_g_q, mla_g_kv, mla_w_uq, mla_w_ukv, mla_w_o,
           router_w, router_b, moe_w_gate, moe_w_up, moe_w_down):
    xt = jnp.concatenate([x.reshape(N_LAT, D_MODEL), ctx.reshape(N_CTX, D_MODEL)], axis=0)
    src8 = jnp.concatenate([c, c_ctx[None], jnp.zeros((8 - BATCH - 1, D_MODEL), F32)], axis=0)
    mod5 = _ada_all(src8, ada_w, ada_b).reshape(DEPTH, 8, 6, 1, D_MODEL)
    norm_g4 = norm_g.reshape(DEPTH, 2, 1, D_MODEL)
    router_wt = router_w.T
    rope_cos, rope_sin = _rope_tables(TM_PROJ)
    final_g4 = final_g.reshape(1, 1, 1, D_MODEL)
    h1 = _norm1(xt, norm_g4, mod5, 0)
    for i in range(DEPTH):
        need_ctx = i < DEPTH - 1
        kind, j = i % N_MIXERS, i // N_MIXERS
        if kind == 0:
            x_mix = _attn_mixer(h1, xt, mod5, i, j, attn_w_qkv, attn_sink, attn_w_o, rope_cos, rope_sin, need_ctx)
        elif kind == 1:
            x_mix = _fnet_mixer(h1, xt, mod5, i, j, fnet_w_o, need_ctx)
        else:
            x_mix = _mla_mixer(h1, xt, mod5, i, j, mla_w_in, mla_g_q, mla_g_kv, mla_w_uq, mla_w_ukv, mla_w_o,
                               rope_cos, rope_sin, need_ctx)
        moe_args = (x_mix, norm_g4, mod5, router_wt, router_b, moe_w_gate, moe_w_up, moe_w_down, i)
        if need_ctx:
            xt, h1 = _moe_layer(*moe_args, norm_g4, i + 1)
        else:
            out = _moe_layer(*moe_args, final_g4, None)
    return out.reshape(BATCH, SEQ, D_MODEL)
```

```python
import functools
import math

import jax
import jax.numpy as jnp
import numpy as np
from jax import lax
from jax.experimental import pallas as pl
from jax.experimental.pallas import tpu as pltpu

D_MODEL = 2048
BATCH = 2
SEQ = 4096
DEPTH = 4
GRID_W = 64
CTX_LEN = 256
N_MIXERS = 3

BLK = 128
A_HQ = 32
A_HKV = 4
A_GROUPS = A_HQ // A_HKV
A_DH = 64
A_SCALE = A_DH ** -0.5

F_GROUPS = 8
F_GROUP_DIM = D_MODEL // F_GROUPS

M_H = 16
M_QR = 512
M_KVR = 512
M_NOPE = 128
M_ROPE = 64
M_V = 128
M_SCALE = (M_NOPE + M_ROPE) ** -0.5

N_EXPERTS = 16
N_EXPERT_GROUPS = 4
EXPERTS_PER_GROUP = N_EXPERTS // N_EXPERT_GROUPS
D_FF = 1408

ROPE_THETA = 10000.0
NORM_EPS = 1e-6
NEG_INF = -1e30

N_LAT = BATCH * SEQ
N_CTX = BATCH * CTX_LEN
N_TOK = N_LAT + N_CTX

LANES = 128
TM = 512
TM_PROJ = 1024
ROW_BLK = 256
ITEM_ROWS = 2048
N_ITEMS = N_EXPERTS + (2 * N_TOK) // ITEM_ROWS
FF_TILE = 128
DOWN_TILE = 256
VMEM_LIMIT = 56 * 1024 * 1024

F32 = jnp.float32
BF16 = jnp.bfloat16


def _cparams(sem):
    return pltpu.CompilerParams(dimension_semantics=sem, vmem_limit_bytes=VMEM_LIMIT)


def _row_group(i, n_lat_tiles, tiles_per_batch):
    return jnp.where(i < n_lat_tiles, i // tiles_per_batch, BATCH)


def _ada_kernel(src_ref, w_ref, b_ref, o_ref):
    s = src_ref[...]
    s = (s * jax.nn.sigmoid(s)).astype(BF16)
    acc = jnp.dot(s, w_ref[...].astype(BF16), preferred_element_type=F32)
    o_ref[...] = acc + b_ref[...]


def _ada_all(src8, ada_w, ada_b):
    tn = 1024
    n6 = ada_w.shape[-1]
    return pl.pallas_call(
        _ada_kernel,
        grid=(DEPTH, n6 // tn),
        in_specs=[
            pl.BlockSpec((8, D_MODEL), lambda l, j: (0, 0)),
            pl.BlockSpec((None, D_MODEL, tn), lambda l, j: (l, 0, j)),
            pl.BlockSpec((None, 1, tn), lambda l, j: (l, 0, j)),
        ],
        out_specs=pl.BlockSpec((None, 8, tn), lambda l, j: (l, 0, j)),
        out_shape=jax.ShapeDtypeStruct((DEPTH, 8, n6), F32),
        compiler_params=_cparams(("arbitrary", "arbitrary")),
        name="ada_ln",
    )(src8, ada_w, ada_b.reshape(DEPTH, 1, n6))


def _mod_spec(layer, which, n_lat_tiles, tiles_per_batch, ngrid):
    def imap(*idx):
        return (layer, _row_group(idx[0], n_lat_tiles, tiles_per_batch), which, 0, 0)
    del ngrid
    return pl.BlockSpec((None, None, None, 1, D_MODEL), imap)


def _norm_mod(x, g, sh, sc):
    y = x * lax.rsqrt(jnp.mean(x * x, axis=-1, keepdims=True) + NORM_EPS) * g
    return y * (1.0 + sc) + sh


def _norm1_kernel(x_ref, g_ref, sh_ref, sc_ref, o_ref):
    o_ref[...] = _norm_mod(x_ref[...], g_ref[...], sh_ref[...], sc_ref[...]).astype(o_ref.dtype)


def _norm1(x, norm_g4, mod5, layer):
    n = x.shape[0]
    nlt, tpb = N_LAT // TM, SEQ // TM
    return pl.pallas_call(
        _norm1_kernel,
        grid=(n // TM,),
        in_specs=[
            pl.BlockSpec((TM, D_MODEL), lambda i: (i, 0)),
            pl.BlockSpec((None, None, 1, D_MODEL), lambda i: (layer, 0, 0, 0)),
            _mod_spec(layer, 0, nlt, tpb, 1),
            _mod_spec(layer, 1, nlt, tpb, 1),
        ],
        out_specs=pl.BlockSpec((TM, D_MODEL), lambda i: (i, 0)),
        out_shape=jax.ShapeDtypeStruct((n, D_MODEL), BF16),
        compiler_params=_cparams(("arbitrary",)),
        name="norm1",
    )(x, norm_g4, mod5, mod5)


def _norm2_kernel(x_ref, g_ref, sh_ref, sc_ref, rw_ref, hp_ref, lg_ref):
    h = _norm_mod(x_ref[...], g_ref[...], sh_ref[...], sc_ref[...])
    lg_ref[...] = lax.dot_general(rw_ref[...], h, (((1,), (1,)), ((), ())),
                                  precision=lax.Precision.HIGHEST, preferred_element_type=F32)
    half = D_MODEL // 2
    hp_ref[...] = pltpu.pack_elementwise([h[:, :half], h[:, half:]], packed_dtype=BF16)


def _norm2(x, norm_g4, mod5, router_wt, layer):
    n = x.shape[0]
    nlt, tpb = N_LAT // TM, SEQ // TM
    return pl.pallas_call(
        _norm2_kernel,
        grid=(n // TM,),
        in_specs=[
            pl.BlockSpec((TM, D_MODEL), lambda i: (i, 0)),
            pl.BlockSpec((None, None, 1, D_MODEL), lambda i: (layer, 1, 0, 0)),
            _mod_spec(layer, 3, nlt, tpb, 1),
            _mod_spec(layer, 4, nlt, tpb, 1),
            pl.BlockSpec((N_EXPERTS, D_MODEL), lambda i: (0, 0)),
        ],
        out_specs=[
            pl.BlockSpec((TM, D_MODEL // 2), lambda i: (i, 0)),
            pl.BlockSpec((N_EXPERTS, TM), lambda i: (0, i)),
        ],
        out_shape=[
            jax.ShapeDtypeStruct((n, D_MODEL // 2), jnp.uint32),
            jax.ShapeDtypeStruct((N_EXPERTS, n), F32),
        ],
        compiler_params=_cparams(("arbitrary",)),
        name="norm2_router_logits",
    )(x, norm_g4, mod5, mod5, router_wt)


def _mm_kernel(*refs, nk, n_extra, epilogue):
    a_ref, w_ref = refs[0], refs[1]
    extra = refs[2:2 + n_extra]
    o_ref = refs[2 + n_extra]
    part = jnp.dot(a_ref[...].astype(BF16), w_ref[...].astype(BF16), preferred_element_type=F32)
    if nk == 1:
        epilogue(part, o_ref, *extra)
        return
    acc_ref = refs[3 + n_extra]
    k = pl.program_id(2)

    @pl.when(k == 0)
    def _():
        acc_ref[...] = part

    @pl.when(k > 0)
    def _():
        acc_ref[...] += part

    @pl.when(k == nk - 1)
    def _():
        epilogue(acc_ref[...], o_ref, *extra)


def _matmul(a, a_spec, w, w_spec, extras, out_shape, out_spec, grid, epilogue, tm, tn, name):
    nk = grid[2]
    kern = functools.partial(_mm_kernel, nk=nk, n_extra=len(extras), epilogue=epilogue)
    scratch = [pltpu.VMEM((tm, tn), F32)] if nk > 1 else []
    return pl.pallas_call(
        kern,
        grid=grid,
        in_specs=[a_spec, w_spec] + [s for _, s in extras],
        out_specs=out_spec,
        out_shape=out_shape,
        scratch_shapes=scratch,
        compiler_params=_cparams(("arbitrary", "arbitrary", "arbitrary")),
        name=name,
    )(a, w, *[x for x, _ in extras])


def _epi_plain(acc, o_ref):
    o_ref[...] = acc.astype(o_ref.dtype)


def _epi_resid_gate(acc, o_ref, x_ref, gate_ref):
    o_ref[...] = x_ref[...] + gate_ref[...] * acc


def _epi_rmsnorm(acc, o_ref, g_ref):
    y = acc * lax.rsqrt(jnp.mean(acc * acc, axis=-1, keepdims=True) + NORM_EPS) * g_ref[...]
    o_ref[...] = y.astype(o_ref.dtype)


def _rope_rotate(acc, cos_ref, sin_ref):
    tn = acc.shape[1]
    reps = tn // LANES
    cos = jnp.concatenate([cos_ref[...]] * reps, axis=1) if reps > 1 else cos_ref[...]
    sin = jnp.concatenate([sin_ref[...]] * reps, axis=1) if reps > 1 else sin_ref[...]
    up = pltpu.roll(acc, tn - 16, 1)
    dn = pltpu.roll(acc, 16, 1)
    lane = lax.broadcasted_iota(jnp.int32, acc.shape, 1)
    rot = jnp.where((lane % 32) < 16, up, dn)
    return acc * cos + rot * sin


def _epi_rope(acc, o_ref, cos_ref, sin_ref, *, rope_cols, scale_cols, scale, n_cols):
    tn = acc.shape[1]
    j = pl.program_id(0)

    def finish(r):
        if scale_cols:
            col = j * tn + lax.broadcasted_iota(jnp.int32, acc.shape, 1)
            r = jnp.where(col < scale_cols, r * scale, r)
        o_ref[...] = r.astype(o_ref.dtype)

    def roped():
        col = j * tn + lax.broadcasted_iota(jnp.int32, acc.shape, 1)
        r = _rope_rotate(acc, cos_ref, sin_ref)
        finish(jnp.where((col >= rope_cols[0]) & (col < rope_cols[1]), r, acc))

    if rope_cols[0] < tn and rope_cols[1] > n_cols - tn:
        roped()
        return
    touches_rope = (j * tn < rope_cols[1]) & ((j + 1) * tn > rope_cols[0])
    pl.when(touches_rope)(roped)
    pl.when(jnp.logical_not(touches_rope))(lambda: finish(acc))


def _rope_tables(tm):
    d_axis = A_DH // 2
    inv_freq = ROPE_THETA ** (-jnp.arange(0, d_axis, 2, dtype=F32) / d_axis)
    t = jnp.arange(SEQ, dtype=jnp.int32)
    ang_r = (t // GRID_W).astype(F32)[:, None] * inv_freq
    ang_c = (t % GRID_W).astype(F32)[:, None] * inv_freq
    ang = jnp.concatenate([ang_r, ang_r, ang_c, ang_c] * 2, axis=1)
    sign = jnp.tile(jnp.concatenate([-jnp.ones(16, F32), jnp.ones(16, F32)]), 4)
    cos = jnp.concatenate([jnp.cos(ang), jnp.ones((tm, LANES), F32)], axis=0)
    sin = jnp.concatenate([jnp.sin(ang) * sign, jnp.zeros((tm, LANES), F32)], axis=0)
    return cos, sin


def _rope_spec(tm):
    nlt, tpb = N_LAT // tm, SEQ // tm
    return pl.BlockSpec((tm, LANES), lambda j, i, k: (jnp.where(i < nlt, i % tpb, tpb), 0))


def _proj(a, a_col, w, layer, n_out, tn, epilogue, extras, out_dtype, name):
    n = a.shape[0]
    kdim = w.shape[-2]
    grid = (n_out // tn, pl.cdiv(n, TM_PROJ), 1)
    if w.ndim == 3:
        w_spec = pl.BlockSpec((None, kdim, tn), lambda j, i, k: (layer, 0, j))
    else:
        w_spec = pl.BlockSpec((kdim, tn), lambda j, i, k: (0, j))
    return _matmul(
        a, pl.BlockSpec((TM_PROJ, kdim), lambda j, i, k: (i, a_col)), w, w_spec, extras,
        jax.ShapeDtypeStruct((n, n_out), out_dtype), pl.BlockSpec((TM_PROJ, tn), lambda j, i, k: (i, j)),
        grid, epilogue, TM_PROJ, tn, name)


def _out_proj_resid(o, w3, layer_w, x, mod5, layer, name):
    nlt, tpb = N_LAT // TM_PROJ, SEQ // TM_PROJ
    tn = 1024
    gate_spec = pl.BlockSpec(
        (None, None, None, 1, tn),
        lambda j, i, k: (layer, _row_group(i, nlt, tpb), 2, 0, j))
    extras = [(x, pl.BlockSpec((TM_PROJ, tn), lambda j, i, k: (i, j))), (mod5, gate_spec)]
    return _proj(o, 0, w3, layer_w, D_MODEL, tn, _epi_resid_gate, extras, F32, name)


def _dup_half(t, upper):
    h = t[:, A_DH:] if upper else t[:, :A_DH]
    return jnp.concatenate([h, h], axis=1)


def _gqa_kernel(sink_ref, q_ref, kp_ref, ks_ref, kn_ref, vp_ref, vs_ref, vn_ref, kc_ref, vc_ref, o_ref,
                *, n_lat_blocks, blocks_per_seq):
    n = pl.program_id(0)
    is_lat = n < n_lat_blocks
    nb = n % blocks_per_seq
    iq = lax.broadcasted_iota(jnp.int32, (BLK, 3 * BLK), 0)
    ik = lax.broadcasted_iota(jnp.int32, (BLK, 3 * BLK), 1)
    mask = (ik >= iq) & (ik <= iq + 2 * BLK)
    mask = mask & ((nb > 0) | (ik >= BLK)) & ((nb < blocks_per_seq - 1) | (ik < 2 * BLK)) & is_lat
    lane = lax.broadcasted_iota(jnp.int32, (BLK, LANES), 1)
    lo_half = lane < A_DH
    keep_lo = lo_half.astype(F32).astype(BF16)
    keep_hi = (1.0 - lo_half.astype(F32)).astype(BF16)
    n_pairs = A_GROUPS // 2
    for hk in range(A_HKV):
        c0 = (hk // 2) * LANES
        upper = (hk % 2) == 1
        k_all = jnp.concatenate([
            _dup_half(kp_ref[:, c0:c0 + LANES], upper), _dup_half(ks_ref[:, c0:c0 + LANES], upper),
            _dup_half(kn_ref[:, c0:c0 + LANES], upper), _dup_half(kc_ref[:, c0:c0 + LANES], upper)], axis=0)
        v_all = jnp.concatenate([
            _dup_half(vp_ref[:, c0:c0 + LANES], upper), _dup_half(vs_ref[:, c0:c0 + LANES], upper),
            _dup_half(vn_ref[:, c0:c0 + LANES], upper), _dup_half(vc_ref[:, c0:c0 + LANES], upper)], axis=0)
        ones = jnp.ones(v_all.shape, BF16)
        v_lo = v_all * keep_lo[0:1] + ones * keep_hi[0:1]
        v_hi = v_all * keep_hi[0:1] + ones * keep_lo[0:1]
        lhs = []
        for p in range(n_pairs):
            qp = q_ref[:, hk * A_GROUPS * A_DH + p * LANES: hk * A_GROUPS * A_DH + (p + 1) * LANES]
            lhs.append(qp * keep_lo)
            lhs.append(qp * keep_hi)
        lhs = jnp.concatenate(lhs, axis=0)
        s = lax.dot_general(lhs, k_all, (((1,), (1,)), ((), ())), preferred_element_type=F32)
        probs, sink_terms = [], []
        for g in range(A_GROUPS):
            sg = s[g * BLK:(g + 1) * BLK]
            s_loc = jnp.where(mask, sg[:, :3 * BLK], NEG_INF)
            s_ctx = sg[:, 3 * BLK:]
            sink = sink_ref[hk * A_GROUPS + g]
            m = jnp.maximum(jnp.maximum(jnp.max(s_loc, axis=-1, keepdims=True),
                                        jnp.max(s_ctx, axis=-1, keepdims=True)), sink)
            sink_terms.append(jnp.exp(sink - m))
            probs.append(jnp.exp(jnp.concatenate([s_loc - m, s_ctx - m], axis=1).astype(BF16)))
        res_lo = jnp.dot(jnp.concatenate(probs[0::2], axis=0), v_lo, preferred_element_type=F32)
        res_hi = jnp.dot(jnp.concatenate(probs[1::2], axis=0), v_hi, preferred_element_type=F32)
        for p in range(n_pairs):
            r_lo = res_lo[p * BLK:(p + 1) * BLK]
            r_hi = res_hi[p * BLK:(p + 1) * BLK]
            o_lo = r_lo / (r_lo[:, A_DH:A_DH + 1] + sink_terms[2 * p])
            o_hi = r_hi / (r_hi[:, 0:1] + sink_terms[2 * p + 1])
            col = hk * A_GROUPS * A_DH + p * LANES
            o_ref[:, col:col + LANES] = jnp.where(lo_half, o_lo, o_hi).astype(o_ref.dtype)


def _gqa_attention(qkv, sink, need_ctx):
    bps = SEQ // BLK
    nlb = BATCH * bps
    n_ctx_blocks = N_CTX // BLK if need_ctx else 0
    nq = nlb + n_ctx_blocks
    kcol, vcol = (A_HQ * A_DH) // 256, (A_HQ * A_DH) // 256 + 1
    cpb = CTX_LEN // BLK

    def batch_of(n):
        return jnp.where(n < nlb, n // bps, (n - nlb) // cpb)

    def loc_map(delta, col):
        def imap(n, s):
            nb = n % bps
            blk = jnp.clip(nb + delta, 0, bps - 1)
            return (jnp.where(n < nlb, (n // bps) * bps + blk, 0), col)
        return imap

    def ctx_map(col):
        return lambda n, s: (N_LAT // CTX_LEN + batch_of(n), col)

    kv_blk = (BLK, 256)
    grid_spec = pltpu.PrefetchScalarGridSpec(
        num_scalar_prefetch=1,
        grid=(nq,),
        in_specs=[
            pl.BlockSpec((BLK, A_HQ * A_DH), lambda n, s: (n, 0)),
            pl.BlockSpec(kv_blk, loc_map(-1, kcol)), pl.BlockSpec(kv_blk, loc_map(0, kcol)),
            pl.BlockSpec(kv_blk, loc_map(1, kcol)),
            pl.BlockSpec(kv_blk, loc_map(-1, vcol)), pl.BlockSpec(kv_blk, loc_map(0, vcol)),
            pl.BlockSpec(kv_blk, loc_map(1, vcol)),
            pl.BlockSpec((CTX_LEN, 256), ctx_map(kcol)), pl.BlockSpec((CTX_LEN, 256), ctx_map(vcol)),
        ],
        out_specs=pl.BlockSpec((BLK, A_HQ * A_DH), lambda n, s: (n, 0)),
    )
    kern = functools.partial(_gqa_kernel, n_lat_blocks=nlb, blocks_per_seq=bps)
    return pl.pallas_call(
        kern,
        grid_spec=grid_spec,
        out_shape=jax.ShapeDtypeStruct((nq * BLK, A_HQ * A_DH), BF16),
        compiler_params=_cparams(("arbitrary",)),
        name="gqa_window_attn",
    )(sink, qkv, qkv, qkv, qkv, qkv, qkv, qkv, qkv, qkv)


MLA_TQ = 1024
MLA_TK = 512


def _mla_kernel(*refs, n_lat_keys):
    if n_lat_keys:
        (qn_ref, qr_ref, kn_ref, v_ref, kr_ref, knc_ref, vc_ref, krc_ref, o_ref, kcat, vcat) = refs
    else:
        (qn_ref, qr_ref, knc_ref, vc_ref, krc_ref, o_ref, kcat, vcat) = refs
    qi = pl.program_id(2)
    nk = n_lat_keys + CTX_LEN

    @pl.when(qi == 0)
    def _():
        if n_lat_keys:
            kcat[0:n_lat_keys, 0:LANES] = kn_ref[...]
            kcat[0:n_lat_keys, LANES:2 * LANES] = kr_ref[...]
            vcat[0:n_lat_keys, 0:M_V] = v_ref[...]
        kcat[n_lat_keys:nk, 0:LANES] = knc_ref[...]
        kcat[n_lat_keys:nk, LANES:2 * LANES] = krc_ref[...]
        vcat[n_lat_keys:nk, 0:M_V] = vc_ref[...]
        vcat[:, M_V:M_V + LANES] = jnp.ones((nk, LANES), BF16)

    q = jnp.concatenate([qn_ref[...], qr_ref[...]], axis=1)

    def chunk(k0, size):
        kc = kcat[k0:k0 + size, :]
        vc = vcat[k0:k0 + size, :]
        s = lax.dot_general(q, kc, (((1,), (1,)), ((), ())), preferred_element_type=F32)
        m_c = jnp.max(s, axis=-1, keepdims=True)
        p = jnp.exp((s - m_c).astype(BF16))
        ol = jnp.dot(p, vc, preferred_element_type=F32)
        return m_c, ol[:, M_V:M_V + 1], ol[:, :M_V]

    parts = [chunk(c * MLA_TK, MLA_TK) for c in range(n_lat_keys // MLA_TK)]
    parts.append(chunk(n_lat_keys, CTX_LEN))
    m = parts[0][0]
    for m_c, _, _ in parts[1:]:
        m = jnp.maximum(m, m_c)
    l = None
    acc = None
    for m_c, l_c, o_c in parts:
        w_c = jnp.exp(m_c - m)
        l = w_c * l_c if l is None else l + w_c * l_c
        acc = w_c * o_c if acc is None else acc + w_c * o_c
    o_ref[...] = (acc / l).astype(o_ref.dtype)


def _mla_attention(qcat, kv, kr, latent):
    hq = M_H
    ctx_row_blk = N_LAT // CTX_LEN
    if latent:
        tq, nqt, n_lat_keys = MLA_TQ, SEQ // MLA_TQ, SEQ
        row0 = lambda b, qi: b * nqt + qi
        n_out = N_LAT
    else:
        tq, nqt, n_lat_keys = CTX_LEN, 1, 0
        row0 = lambda b, qi: ctx_row_blk + b
        n_out = N_CTX
    in_specs = [
        pl.BlockSpec((tq, LANES), lambda b, h, qi: (row0(b, qi), h)),
        pl.BlockSpec((tq, LANES), lambda b, h, qi: (row0(b, qi), hq + h)),
    ]
    args = [qcat, qcat]
    if latent:
        in_specs += [
            pl.BlockSpec((SEQ, LANES), lambda b, h, qi: (b, 2 * h)),
            pl.BlockSpec((SEQ, LANES), lambda b, h, qi: (b, 2 * h + 1)),
            pl.BlockSpec((SEQ, LANES), lambda b, h, qi: (b, 0)),
        ]
        args += [kv, kv, kr]
    in_specs += [
        pl.BlockSpec((CTX_LEN, LANES), lambda b, h, qi: (ctx_row_blk + b, 2 * h)),
        pl.BlockSpec((CTX_LEN, LANES), lambda b, h, qi: (ctx_row_blk + b, 2 * h + 1)),
        pl.BlockSpec((CTX_LEN, LANES), lambda b, h, qi: (ctx_row_blk + b, 0)),
    ]
    args += [kv, kv, kr]
    if latent:
        out_spec = pl.BlockSpec((tq, LANES), lambda b, h, qi: (b * nqt + qi, h))
    else:
        out_spec = pl.BlockSpec((tq, LANES), lambda b, h, qi: (b, h))
    nk = n_lat_keys + CTX_LEN
    return pl.pallas_call(
        functools.partial(_mla_kernel, n_lat_keys=n_lat_keys),
        grid=(BATCH, hq, nqt),
        in_specs=in_specs,
        out_specs=out_spec,
        out_shape=jax.ShapeDtypeStruct((n_out, hq * M_V), BF16),
        scratch_shapes=[pltpu.VMEM((nk, 2 * LANES), BF16), pltpu.VMEM((nk, M_V + LANES), BF16)],
        compiler_params=_cparams(("arbitrary", "arbitrary", "arbitrary")),
        name="mla_attn_lat" if latent else "mla_attn_ctx",
    )(*args)


def _dft_mats(n, scale):
    sub = 64
    k = jnp.arange(n, dtype=jnp.int32)
    j1 = jnp.arange(n // sub, dtype=jnp.int32) * sub
    j0 = jnp.arange(sub, dtype=jnp.int32)
    a = ((j1[:, None] * k[None, :]) % n).astype(F32) * (2.0 * math.pi / n)
    b = ((j0[:, None] * k[None, :]) % n).astype(F32) * (2.0 * math.pi / n)
    ca, sa = (jnp.cos(a) * scale)[:, None, :], (jnp.sin(a) * scale)[:, None, :]
    cb, sb = jnp.cos(b)[None, :, :], jnp.sin(b)[None, :, :]
    return (ca * cb - sa * sb).reshape(n, n), (sa * cb + ca * sb).reshape(n, n)


def _fnet_stage1_kernel(h_ref, w_ref, z_ref):
    z = jnp.dot(h_ref[...], w_ref[...], preferred_element_type=F32)
    z_ref[0] = z[:, :F_GROUP_DIM].astype(z_ref.dtype)
    z_ref[1] = z[:, F_GROUP_DIM:].astype(z_ref.dtype)


def _fnet_stage1(h1, w1, row_blk0, seq_len, nbatch):
    tm = min(TM, seq_len)
    tiles = seq_len // tm
    return pl.pallas_call(
        _fnet_stage1_kernel,
        grid=(nbatch, tiles, F_GROUPS),
        in_specs=[
            pl.BlockSpec((tm, F_GROUP_DIM), lambda b, i, g: (row_blk0 + b * tiles + i, g)),
            pl.BlockSpec((F_GROUP_DIM, 2 * F_GROUP_DIM), lambda b, i, g: (0, 0)),
        ],
        out_specs=pl.BlockSpec((None, 2, tm, F_GROUP_DIM), lambda b, i, g: (b, 0, i, g)),
        out_shape=jax.ShapeDtypeStruct((nbatch, 2, seq_len, D_MODEL), BF16),
        compiler_params=_cparams(("arbitrary", "arbitrary", "arbitrary")),
        name="fnet_channel_dft",
    )(h1, w1)


def _fnet_stage2_kernel(a_ref, z_ref, o_ref, acc_ref, *, nk):
    k = pl.program_id(2)
    part = jnp.dot(a_ref[...], z_ref[...], preferred_element_type=F32)

    @pl.when(k == 0)
    def _():
        acc_ref[...] = part

    @pl.when(k > 0)
    def _():
        acc_ref[...] += part

    @pl.when(k == nk - 1)
    def _():
        o_ref[...] = acc_ref[...].astype(o_ref.dtype)


def _fnet_stage2(a2, z, seq_len, nbatch):
    tm = min(1024, seq_len)
    tk = min(1024, seq_len)
    nk = 2 * seq_len // tk
    z2 = z.reshape(nbatch, 2 * seq_len, D_MODEL)
    return pl.pallas_call(
        functools.partial(_fnet_stage2_kernel, nk=nk),
        grid=(nbatch, seq_len // tm, nk),
        in_specs=[
            pl.BlockSpec((tm, tk), lambda b, i, k: (i, k)),
            pl.BlockSpec((None, tk, D_MODEL), lambda b, i, k: (b, k, 0)),
        ],
        out_specs=pl.BlockSpec((None, tm, D_MODEL), lambda b, i, k: (b, i, 0)),
        out_shape=jax.ShapeDtypeStruct((nbatch, seq_len, D_MODEL), BF16),
        scratch_shapes=[pltpu.VMEM((tm, D_MODEL), F32)],
        compiler_params=_cparams(("arbitrary", "arbitrary", "arbitrary")),
        name="fnet_position_dft",
    )(a2, z2)


ROUTER_TILE = 512


def _top2_rows(v, rowf):
    big = float(v.shape[0])
    m1 = jnp.max(v, axis=0, keepdims=True)
    i1 = jnp.min(jnp.where(v == m1, rowf, big), axis=0, keepdims=True)
    v2 = jnp.where(rowf == i1, NEG_INF, v)
    m2 = jnp.max(v2, axis=0, keepdims=True)
    i2 = jnp.min(jnp.where(v2 == m2, rowf, big), axis=0, keepdims=True)
    return m1, i1, m2, i2


def _router_kernel(lg_ref, b_ref, idx_ref, w_ref, cnt_ref, carry_ref):
    step = pl.program_id(0)
    t = lg_ref.shape[1]

    @pl.when(step == 0)
    def _():
        carry_ref[...] = jnp.zeros_like(carry_ref)

    scores = jax.nn.sigmoid(lg_ref[...])
    sel = scores + b_ref[...][:, 0:1]
    row = lax.broadcasted_iota(jnp.int32, (N_EXPERTS, t), 0)
    rowf = row.astype(F32)
    grp = row // EXPERTS_PER_GROUP
    best = jnp.zeros((1, t), jnp.int32)
    best_v = None
    for g in range(N_EXPERT_GROUPS):
        m1, _, m2, _ = _top2_rows(jnp.where(grp == g, sel, NEG_INF), rowf)
        gs = m1 + m2
        if g == 0:
            best_v = gs
        else:
            upd = gs > best_v
            best = jnp.where(upd, g, best)
            best_v = jnp.where(upd, gs, best_v)
    _, e0, _, e1 = _top2_rows(jnp.where(grp == best, sel, NEG_INF), rowf)
    oh0 = rowf == e0
    oh1 = rowf == e1
    s0 = jnp.sum(jnp.where(oh0, scores, 0.0), axis=0, keepdims=True)
    s1 = jnp.sum(jnp.where(oh1, scores, 0.0), axis=0, keepdims=True)
    tot = s0 + s1
    cnt = jnp.where(oh0 | oh1, 1.0, 0.0)
    a = lax.broadcasted_iota(jnp.int32, (t, t), 0)
    bcol = lax.broadcasted_iota(jnp.int32, (t, t), 1)
    tri = jnp.where(a < bcol, 1.0, 0.0).astype(BF16)
    excl = jnp.dot(cnt.astype(BF16), tri, preferred_element_type=F32) + carry_ref[...][:, 0:1]
    r0 = jnp.sum(jnp.where(oh0, excl, 0.0), axis=0, keepdims=True)
    r1 = jnp.sum(jnp.where(oh1, excl, 0.0), axis=0, keepdims=True)
    new_carry = carry_ref[...] + jnp.sum(cnt, axis=1, keepdims=True)
    carry_ref[...] = new_carry
    idx_ref[...] = jnp.zeros(idx_ref.shape, jnp.int32)
    idx_ref[0:1, :] = e0.astype(jnp.int32)
    idx_ref[1:2, :] = e1.astype(jnp.int32)
    idx_ref[2:3, :] = r0.astype(jnp.int32)
    idx_ref[3:4, :] = r1.astype(jnp.int32)
    w_ref[...] = jnp.zeros(w_ref.shape, F32)
    w_ref[0:1, :] = s0 / tot
    w_ref[1:2, :] = s1 / tot
    cnt_ref[...] = new_carry.astype(jnp.int32)


def _router(logits_t, router_b):
    n = logits_t.shape[1]
    t = ROUTER_TILE
    b2 = jnp.broadcast_to(router_b.astype(F32)[:, None], (N_EXPERTS, LANES))
    return pl.pallas_call(
        _router_kernel,
        grid=(n // t,),
        in_specs=[pl.BlockSpec((N_EXPERTS, t), lambda i: (0, i)),
                  pl.BlockSpec((N_EXPERTS, LANES), lambda i: (0, 0))],
        out_specs=[pl.BlockSpec((8, t), lambda i: (0, i)),
                   pl.BlockSpec((8, t), lambda i: (0, i)),
                   pl.BlockSpec((N_EXPERTS, LANES), lambda i: (0, 0))],
        out_shape=[jax.ShapeDtypeStruct((8, n), jnp.int32),
                   jax.ShapeDtypeStruct((8, n), F32),
                   jax.ShapeDtypeStruct((N_EXPERTS, LANES), jnp.int32)],
        scratch_shapes=[pltpu.VMEM((N_EXPERTS, LANES), F32)],
        compiler_params=_cparams(("arbitrary",)),
        name="moe_router",
    )(logits_t, b2)


def _plan_items(counts):
    n_chunks = (counts + ITEM_ROWS - 1) // ITEM_ROWS
    ends = jnp.cumsum(n_chunks)
    starts = ends - n_chunks
    base = starts * ITEM_ROWS
    n_valid = ends[-1]
    item = jnp.arange(N_ITEMS, dtype=jnp.int32)
    item_src = jnp.minimum(item, n_valid - 1)
    expert = jnp.sum((ends[None, :] <= item_src[:, None]).astype(jnp.int32), axis=1)
    rows = jnp.clip(counts[expert] - (item_src - starts[expert]) * ITEM_ROWS, 0, ITEM_ROWS)
    rows = jnp.where(item < n_valid, rows, 0).astype(jnp.int32)
    nblk = (rows + ROW_BLK - 1) // ROW_BLK
    return base.astype(jnp.int32), item_src.astype(jnp.int32), expert, nblk, rows


DISPATCH_TILE = 256


def _dispatch_kernel(p0_ref, p1_ref, hp_ref, hs_ref, sem):
    base = pl.program_id(0) * DISPATCH_TILE

    def copies(r):
        t = base + r
        return (pltpu.make_async_copy(hp_ref.at[pl.ds(r, 1)], hs_ref.at[pl.ds(p0_ref[t], 1)], sem),
                pltpu.make_async_copy(hp_ref.at[pl.ds(r, 1)], hs_ref.at[pl.ds(p1_ref[t], 1)], sem))

    def start(r, c):
        a, b = copies(r)
        a.start(priority=0)
        b.start(priority=1)
        return c

    def wait(r, c):
        a, b = copies(r)
        a.wait()
        b.wait()
        return c

    lax.fori_loop(0, DISPATCH_TILE, start, 0, unroll=8)
    lax.fori_loop(0, DISPATCH_TILE, wait, 0, unroll=8)


def _dispatch(hp, pos0, pos1):
    n = hp.shape[0]
    grid_spec = pltpu.PrefetchScalarGridSpec(
        num_scalar_prefetch=2,
        grid=(n // DISPATCH_TILE,),
        in_specs=[pl.BlockSpec((DISPATCH_TILE, D_MODEL // 2), lambda i, p0, p1: (i, 0))],
        out_specs=pl.BlockSpec(memory_space=pl.ANY),
        scratch_shapes=[pltpu.SemaphoreType.DMA(())],
    )
    return pl.pallas_call(
        _dispatch_kernel,
        grid_spec=grid_spec,
        out_shape=jax.ShapeDtypeStruct((N_ITEMS * ITEM_ROWS, D_MODEL // 2), hp.dtype),
        compiler_params=pltpu.CompilerParams(dimension_semantics=("arbitrary",), has_side_effects=True),
        name="moe_dispatch",
    )(pos0, pos1, hp)


def _gate_up_kernel(src_ref, exp_ref, nblk_ref, rows_ref, hs_ref, wg_ref, wu_ref, a_ref, hb_ref):
    del src_ref, exp_ref
    i = pl.program_id(0)
    f = pl.program_id(1)
    nblk = nblk_ref[i]
    n_rows = rows_ref[i]
    half = D_MODEL // 2

    @pl.when(nblk > 0)
    def _():
        @pl.when(f == 0)
        def _():
            def unpack(r, c):
                rows = pl.ds(pl.multiple_of(r * ROW_BLK, ROW_BLK), ROW_BLK)
                u = hs_ref[rows, :]
                live = (r * ROW_BLK + lax.broadcasted_iota(jnp.int32, (ROW_BLK, half), 0)) < n_rows
                lo = pltpu.unpack_elementwise(u, index=0, packed_dtype=BF16, unpacked_dtype=F32)
                hi = pltpu.unpack_elementwise(u, index=1, packed_dtype=BF16, unpacked_dtype=F32)
                hb_ref[rows, 0:half] = jnp.where(live, lo, 0.0).astype(BF16)
                hb_ref[rows, half:D_MODEL] = jnp.where(live, hi, 0.0).astype(BF16)
                return c
            lax.fori_loop(0, nblk, unpack, 0)

        for m in range(1, ITEM_ROWS // ROW_BLK + 1):
            rows = m * ROW_BLK

            @pl.when(nblk == m)
            def _():
                w = jnp.concatenate([wg_ref[...].astype(BF16), wu_ref[...].astype(BF16)], axis=1)
                res = jnp.dot(hb_ref[0:rows, :], w, preferred_element_type=F32)
                g = res[:, :FF_TILE]
                u = res[:, FF_TILE:]
                a_ref[0:rows, :] = (g * jax.nn.sigmoid(g) * u).astype(a_ref.dtype)
                if rows < ITEM_ROWS:
                    a_ref[rows:ITEM_ROWS, :] = jnp.zeros((ITEM_ROWS - rows, FF_TILE), a_ref.dtype)


def _gate_up(hs, w_gate, w_up, layer, item_src, expert, nblk, rows):
    nf = D_FF // FF_TILE

    def frozen_f(i, f, nb):
        return jnp.where(nb[i] > 0, f, nf - 1)

    grid_spec = pltpu.PrefetchScalarGridSpec(
        num_scalar_prefetch=4,
        grid=(N_ITEMS, nf),
        in_specs=[
            pl.BlockSpec((ITEM_ROWS, D_MODEL // 2), lambda i, f, src, ex, nb, rw: (src[i], 0)),
            pl.BlockSpec((None, None, D_MODEL, FF_TILE),
                         lambda i, f, src, ex, nb, rw: (layer, ex[i], 0, frozen_f(i, f, nb))),
            pl.BlockSpec((None, None, D_MODEL, FF_TILE),
                         lambda i, f, src, ex, nb, rw: (layer, ex[i], 0, frozen_f(i, f, nb))),
        ],
        out_specs=pl.BlockSpec((ITEM_ROWS, FF_TILE),
                               lambda i, f, src, ex, nb, rw: (src[i], frozen_f(i, f, nb))),
        scratch_shapes=[pltpu.VMEM((ITEM_ROWS, D_MODEL), BF16)],
    )
    return pl.pallas_call(
        _gate_up_kernel,
        grid_spec=grid_spec,
        out_shape=jax.ShapeDtypeStruct((N_ITEMS * ITEM_ROWS, D_FF), BF16),
        compiler_params=_cparams(("arbitrary", "arbitrary")),
        name="moe_gate_up",
    )(item_src, expert, nblk, rows, hs, w_gate, w_up)


def _down_kernel(src_ref, exp_ref, nblk_ref, a_ref, wlo_ref, whi_ref, y_ref):
    del src_ref, exp_ref
    i = pl.program_id(0)
    nblk = nblk_ref[i]

    for m in range(1, ITEM_ROWS // ROW_BLK + 1):
        rows = m * ROW_BLK

        @pl.when(nblk == m)
        def _():
            w = jnp.concatenate([wlo_ref[...].astype(BF16), whi_ref[...].astype(BF16)], axis=1)
            res = jnp.dot(a_ref[0:rows, :], w, preferred_element_type=F32)
            y_ref[0:rows, :] = pltpu.pack_elementwise([res[:, :DOWN_TILE], res[:, DOWN_TILE:]], packed_dtype=BF16)
            if rows < ITEM_ROWS:
                zero = jnp.zeros((ITEM_ROWS - rows, DOWN_TILE), F32)
                y_ref[rows:ITEM_ROWS, :] = pltpu.pack_elementwise([zero, zero], packed_dtype=BF16)


def _down(a, w_down, layer, item_src, expert, nblk):
    nj = D_MODEL // (2 * DOWN_TILE)

    def frozen_j(i, j, nb):
        return jnp.where(nb[i] > 0, j, nj - 1)

    grid_spec = pltpu.PrefetchScalarGridSpec(
        num_scalar_prefetch=3,
        grid=(N_ITEMS, nj),
        in_specs=[
            pl.BlockSpec((ITEM_ROWS, D_FF), lambda i, j, src, ex, nb: (src[i], 0)),
            pl.BlockSpec((None, None, D_FF, DOWN_TILE),
                         lambda i, j, src, ex, nb: (layer, ex[i], 0, frozen_j(i, j, nb))),
            pl.BlockSpec((None, None, D_FF, DOWN_TILE),
                         lambda i, j, src, ex, nb: (layer, ex[i], 0, nj + frozen_j(i, j, nb))),
        ],
        out_specs=pl.BlockSpec((ITEM_ROWS, DOWN_TILE), lambda i, j, src, ex, nb: (src[i], frozen_j(i, j, nb))),
    )
    return pl.pallas_call(
        _down_kernel,
        grid_spec=grid_spec,
        out_shape=jax.ShapeDtypeStruct((N_ITEMS * ITEM_ROWS, D_MODEL // 2), jnp.uint32),
        compiler_params=_cparams(("arbitrary", "arbitrary")),
        name="moe_down",
    )(item_src, expert, nblk, a, w_down, w_down)


COMBINE_TILE = 256


def _combine_kernel(p0_ref, p1_ref, x_ref, gate_ref, w0_ref, w1_ref, g_ref, sh_ref, sc_ref, ys_ref, *rest, final):
    if final:
        o_ref, y0_ref, y1_ref, sem = rest
    else:
        xo_ref, h_ref, y0_ref, y1_ref, sem = rest
    base = pl.program_id(0) * COMBINE_TILE

    def copies(r):
        t = base + r
        return (pltpu.make_async_copy(ys_ref.at[pl.ds(p0_ref[t], 1)], y0_ref.at[pl.ds(r, 1)], sem),
                pltpu.make_async_copy(ys_ref.at[pl.ds(p1_ref[t], 1)], y1_ref.at[pl.ds(r, 1)], sem))

    def start(r, c):
        a, b = copies(r)
        a.start(priority=0)
        b.start(priority=1)
        return c

    def wait(r, c):
        a, b = copies(r)
        a.wait()
        b.wait()
        return c

    lax.fori_loop(0, COMBINE_TILE, start, 0, unroll=8)
    lax.fori_loop(0, COMBINE_TILE, wait, 0, unroll=8)
    half = D_MODEL // 2
    w0 = w0_ref[...]
    w1 = w1_ref[...]
    xs = []
    for hh in range(2):
        cols = slice(hh * half, (hh + 1) * half)
        y = (w0 * pltpu.unpack_elementwise(y0_ref[...], index=hh, packed_dtype=BF16, unpacked_dtype=F32)
             + w1 * pltpu.unpack_elementwise(y1_ref[...], index=hh, packed_dtype=BF16, unpacked_dtype=F32))
        xs.append(x_ref[:, cols] + gate_ref[:, cols] * y)
    ssq = jnp.sum(xs[0] * xs[0], axis=-1, keepdims=True) + jnp.sum(xs[1] * xs[1], axis=-1, keepdims=True)
    inv = lax.rsqrt(ssq * (1.0 / D_MODEL) + NORM_EPS)
    for hh in range(2):
        cols = slice(hh * half, (hh + 1) * half)
        normed = xs[hh] * inv * g_ref[:, cols]
        if final:
            o_ref[:, cols] = normed
        else:
            xo_ref[:, cols] = xs[hh]
            h_ref[:, cols] = (normed * (1.0 + sc_ref[:, cols]) + sh_ref[:, cols]).astype(h_ref.dtype)


def _combine(x, ys, pos0, pos1, w0, w1, mod5, layer, norm_g, next_layer):
    n = x.shape[0]
    t = COMBINE_TILE
    nlt, tpb = N_LAT // t, SEQ // t
    final = next_layer is None
    mod_layer = layer if final else next_layer
    g_layer = 0 if final else next_layer

    def mod_spec(lyr, which):
        return pl.BlockSpec((None, None, None, 1, D_MODEL),
                            lambda i, p0, p1: (lyr, _row_group(i, nlt, tpb), which, 0, 0))

    row_spec = pl.BlockSpec((t, D_MODEL), lambda i, p0, p1: (i, 0))
    grid_spec = pltpu.PrefetchScalarGridSpec(
        num_scalar_prefetch=2,
        grid=(n // t,),
        in_specs=[
            row_spec,
            mod_spec(layer, 5),
            pl.BlockSpec((t, 1), lambda i, p0, p1: (i, 0)),
            pl.BlockSpec((t, 1), lambda i, p0, p1: (i, 0)),
            pl.BlockSpec((None, None, 1, D_MODEL), lambda i, p0, p1: (g_layer, 0, 0, 0)),
            mod_spec(mod_layer, 0),
            mod_spec(mod_layer, 1),
            pl.BlockSpec(memory_space=pl.ANY),
        ],
        out_specs=row_spec if final else [row_spec, row_spec],
        scratch_shapes=[pltpu.VMEM((t, D_MODEL // 2), jnp.uint32), pltpu.VMEM((t, D_MODEL // 2), jnp.uint32),
                        pltpu.SemaphoreType.DMA(())],
    )
    if final:
        out_shape = jax.ShapeDtypeStruct((n, D_MODEL), F32)
    else:
        out_shape = [jax.ShapeDtypeStruct((n, D_MODEL), F32), jax.ShapeDtypeStruct((n, D_MODEL), BF16)]
    return pl.pallas_call(
        functools.partial(_combine_kernel, final=final),
        grid_spec=grid_spec,
        out_shape=out_shape,
        compiler_params=_cparams(("arbitrary",)),
        name="moe_combine_final" if final else "moe_combine",
    )(pos0, pos1, x, mod5, w0, w1, norm_g, mod5, mod5, ys)


def _moe_layer(x, norm_g4, mod5, router_wt, router_b, w_gate, w_up, w_down, layer, out_norm_g, next_layer):
    hp, logits_t = _norm2(x, norm_g4, mod5, router_wt, layer)
    idx, wts, counts = _router(logits_t, router_b)
    base, item_src, expert, nblk, rows = _plan_items(counts[:, 0])
    pos0 = base[idx[0]] + idx[2]
    pos1 = base[idx[1]] + idx[3]
    hs = _dispatch(hp, pos0, pos1)
    a = _gate_up(hs, w_gate, w_up, layer, item_src, expert, nblk, rows)
    ys = _down(a, w_down, layer, item_src, expert, nblk)
    return _combine(x, ys, pos0, pos1, wts[0][:, None], wts[1][:, None], mod5, layer, out_norm_g, next_layer)


def _attn_mixer(h1, x, mod5, layer, j, attn_w_qkv, attn_sink, attn_w_o, rope_cos, rope_sin, need_ctx):
    q_cols = A_HQ * A_DH
    epi = functools.partial(_epi_rope, rope_cols=(0, q_cols + A_HKV * A_DH), scale_cols=q_cols, scale=A_SCALE,
                            n_cols=q_cols + 2 * A_HKV * A_DH)
    extras = [(rope_cos, _rope_spec(TM_PROJ)), (rope_sin, _rope_spec(TM_PROJ))]
    qkv = _proj(h1, 0, attn_w_qkv, j, q_cols + 2 * A_HKV * A_DH, 512, epi, extras, BF16, "attn_qkv_proj")
    o = _gqa_attention(qkv, attn_sink[j].astype(F32), need_ctx)
    return _out_proj_resid(o, attn_w_o, j, x, mod5, layer, "attn_out_proj")


def _fnet_mixer(h1, x, mod5, layer, j, fnet_w_o, need_ctx):
    cc, sc = _dft_mats(F_GROUP_DIM, F_GROUP_DIM ** -0.5)
    w1 = jnp.concatenate([cc, sc], axis=1).astype(BF16)
    outs = []
    for seq_len, row_blk0, active in ((SEQ, 0, True), (CTX_LEN, N_LAT // min(TM, CTX_LEN), need_ctx)):
        if not active:
            continue
        cl, sl = _dft_mats(seq_len, seq_len ** -0.5)
        a2 = jnp.concatenate([cl, -sl], axis=1).astype(BF16)
        z = _fnet_stage1(h1, w1, row_blk0, seq_len, BATCH)
        y = _fnet_stage2(a2, z, seq_len, BATCH)
        outs.append(y.reshape(BATCH * seq_len, D_MODEL))
    y_all = jnp.concatenate(outs, axis=0) if len(outs) > 1 else outs[0]
    return _out_proj_resid(y_all, fnet_w_o, j, x, mod5, layer, "fnet_out_proj")


def _mla_mixer(h1, x, mod5, layer, j, mla_w_in, mla_g_q, mla_g_kv, mla_w_uq, mla_w_ukv, mla_w_o,
               rope_cos, rope_sin, need_ctx):
    g_cat = jnp.concatenate([mla_g_q[j], mla_g_kv[j]]).reshape(1, M_QR + M_KVR)
    extras = [(g_cat, pl.BlockSpec((1, M_QR), lambda jj, i, k: (0, jj)))]
    cqkv = _proj(h1, 0, mla_w_in, j, M_QR + M_KVR, M_QR, _epi_rmsnorm, extras, BF16, "mla_in_proj")
    w_kr = jnp.pad(mla_w_in[j][:, M_QR + M_KVR:], ((0, 0), (0, LANES - M_ROPE)))
    epi_kr = functools.partial(_epi_rope, rope_cols=(0, LANES), scale_cols=0, scale=1.0, n_cols=LANES)
    rope_extras = [(rope_cos, _rope_spec(TM_PROJ)), (rope_sin, _rope_spec(TM_PROJ))]
    kr = _proj(h1, 0, w_kr, 0, LANES, LANES, epi_kr, rope_extras, BF16, "mla_kr_proj")
    w_uq = mla_w_uq[j].reshape(M_QR, M_H, M_NOPE + M_ROPE)
    w_qn = w_uq[:, :, :M_NOPE].reshape(M_QR, M_H * M_NOPE)
    w_qr = jnp.pad(w_uq[:, :, M_NOPE:], ((0, 0), (0, 0), (0, LANES - M_ROPE))).reshape(M_QR, M_H * LANES)
    w_q = jnp.concatenate([w_qn, w_qr], axis=1)
    nope_cols = M_H * M_NOPE
    epi_q = functools.partial(_epi_rope, rope_cols=(nope_cols, nope_cols + M_H * LANES),
                              scale_cols=nope_cols + M_H * LANES, scale=M_SCALE, n_cols=nope_cols + M_H * LANES)
    qcat = _proj(cqkv, 0, w_q, 0, nope_cols + M_H * LANES, 1024, epi_q, rope_extras, BF16, "mla_q_proj")
    kv = _proj(cqkv, 1, mla_w_ukv, j, M_H * (M_NOPE + M_V), 1024, _epi_plain, [], BF16, "mla_kv_proj")
    o = _mla_attention(qcat, kv, kr, latent=True)
    if need_ctx:
        o = jnp.concatenate([o, _mla_attention(qcat, kv, kr, latent=False)], axis=0)
    return _out_proj_resid(o, mla_w_o, j, x, mod5, layer, "mla_out_proj")


def kernel(x, c, ctx, c_ctx, ada_w, ada_b, norm_g, final_g, attn_w_qkv, attn_sink, attn_w_o,
           fnet_w_o, mla_w_in, mla_g_q, mla_g_kv, mla_w_uq, mla_w_ukv, mla_w_o,
           router_w, router_b, moe_w_gate, moe_w_up, moe_w_down):
    xt = jnp.concatenate([x.reshape(N_LAT, D_MODEL), ctx.reshape(N_CTX, D_MODEL)], axis=0)
    src8 = jnp.concatenate([c, c_ctx[None], jnp.zeros((8 - BATCH - 1, D_MODEL), F32)], axis=0)
    mod5 = _ada_all(src8, ada_w, ada_b).reshape(DEPTH, 8, 6, 1, D_MODEL)
    norm_g4 = norm_g.reshape(DEPTH, 2, 1, D_MODEL)
    router_wt = router_w.T
    rope_cos, rope_sin = _rope_tables(TM_PROJ)
    final_g4 = final_g.reshape(1, 1, 1, D_MODEL)
    h1 = _norm1(xt, norm_g4, mod5, 0)
    for i in range(DEPTH):
        need_ctx = i < DEPTH - 1
        kind, j = i % N_MIXERS, i // N_MIXERS
        if kind == 0:
            x_mix = _attn_mixer(h1, xt, mod5, i, j, attn_w_qkv, attn_sink, attn_w_o, rope_cos, rope_sin, need_ctx)
        elif kind == 1:
            x_mix = _fnet_mixer(h1, xt, mod5, i, j, fnet_w_o, need_ctx)
        else:
            x_mix = _mla_mixer(h1, xt, mod5, i, j, mla_w_in, mla_g_q, mla_g_kv, mla_w_uq, mla_w_ukv, mla_w_o,
                               rope_cos, rope_sin, need_ctx)
        moe_args = (x_mix, norm_g4, mod5, router_wt, router_b, moe_w_gate, moe_w_up, moe_w_down, i)
        if need_ctx:
            xt, h1 = _moe_layer(*moe_args, norm_g4, i + 1)
        else:
            out = _moe_layer(*moe_args, final_g4, None)
    return out.reshape(BATCH, SEQ, D_MODEL)
```

```python
import functools
import math

import jax
import jax.numpy as jnp
import numpy as np
from jax import lax
from jax.experimental import pallas as pl
from jax.experimental.pallas import tpu as pltpu

D_MODEL = 2048
BATCH = 2
SEQ = 4096
DEPTH = 4
GRID_W = 64
CTX_LEN = 256
N_MIXERS = 3

BLK = 128
A_HQ = 32
A_HKV = 4
A_GROUPS = A_HQ // A_HKV
A_DH = 64
A_SCALE = A_DH ** -0.5

F_GROUPS = 8
F_GROUP_DIM = D_MODEL // F_GROUPS

M_H = 16
M_QR = 512
M_KVR = 512
M_NOPE = 128
M_ROPE = 64
M_V = 128
M_SCALE = (M_NOPE + M_ROPE) ** -0.5

N_EXPERTS = 16
N_EXPERT_GROUPS = 4
EXPERTS_PER_GROUP = N_EXPERTS // N_EXPERT_GROUPS
D_FF = 1408

ROPE_THETA = 10000.0
NORM_EPS = 1e-6
NEG_INF = -1e30

N_LAT = BATCH * SEQ
N_CTX = BATCH * CTX_LEN
N_TOK = N_LAT + N_CTX

LANES = 128
TM = 512
TM_PROJ = 1024
ROW_BLK = 256
ITEM_ROWS = 2048
N_ITEMS = N_EXPERTS + (2 * N_TOK) // ITEM_ROWS
FF_TILE = 128
DOWN_TILE = 256
VMEM_LIMIT = 56 * 1024 * 1024

F32 = jnp.float32
BF16 = jnp.bfloat16


def _cparams(sem):
    return pltpu.CompilerParams(dimension_semantics=sem, vmem_limit_bytes=VMEM_LIMIT)


def _row_group(i, n_lat_tiles, tiles_per_batch):
    return jnp.where(i < n_lat_tiles, i // tiles_per_batch, BATCH)


def _ada_kernel(src_ref, w_ref, b_ref, o_ref):
    s = src_ref[...]
    s = (s * jax.nn.sigmoid(s)).astype(BF16)
    acc = jnp.dot(s, w_ref[...].astype(BF16), preferred_element_type=F32)
    o_ref[...] = acc + b_ref[...]


def _ada_all(src8, ada_w, ada_b):
    tn = 1024
    n6 = ada_w.shape[-1]
    return pl.pallas_call(
        _ada_kernel,
        grid=(DEPTH, n6 // tn),
        in_specs=[
            pl.BlockSpec((8, D_MODEL), lambda l, j: (0, 0)),
            pl.BlockSpec((None, D_MODEL, tn), lambda l, j: (l, 0, j)),
            pl.BlockSpec((None, 1, tn), lambda l, j: (l, 0, j)),
        ],
        out_specs=pl.BlockSpec((None, 8, tn), lambda l, j: (l, 0, j)),
        out_shape=jax.ShapeDtypeStruct((DEPTH, 8, n6), F32),
        compiler_params=_cparams(("arbitrary", "arbitrary")),
        name="ada_ln",
    )(src8, ada_w, ada_b.reshape(DEPTH, 1, n6))


def _mod_spec(layer, which, n_lat_tiles, tiles_per_batch, ngrid):
    def imap(*idx):
        return (layer, _row_group(idx[0], n_lat_tiles, tiles_per_batch), which, 0, 0)
    del ngrid
    return pl.BlockSpec((None, None, None, 1, D_MODEL), imap)


def _norm_mod(x, g, sh, sc):
    y = x * lax.rsqrt(jnp.mean(x * x, axis=-1, keepdims=True) + NORM_EPS) * g
    return y * (1.0 + sc) + sh


def _norm1_kernel(x_ref, g_ref, sh_ref, sc_ref, o_ref):
    o_ref[...] = _norm_mod(x_ref[...], g_ref[...], sh_ref[...], sc_ref[...]).astype(o_ref.dtype)


def _norm1(x, norm_g4, mod5, layer):
    n = x.shape[0]
    nlt, tpb = N_LAT // TM, SEQ // TM
    return pl.pallas_call(
        _norm1_kernel,
        grid=(n // TM,),
        in_specs=[
            pl.BlockSpec((TM, D_MODEL), lambda i: (i, 0)),
            pl.BlockSpec((None, None, 1, D_MODEL), lambda i: (layer, 0, 0, 0)),
            _mod_spec(layer, 0, nlt, tpb, 1),
            _mod_spec(layer, 1, nlt, tpb, 1),
        ],
        out_specs=pl.BlockSpec((TM, D_MODEL), lambda i: (i, 0)),
        out_shape=jax.ShapeDtypeStruct((n, D_MODEL), BF16),
        compiler_params=_cparams(("arbitrary",)),
        name="norm1",
    )(x, norm_g4, mod5, mod5)


def _norm2_kernel(x_ref, g_ref, sh_ref, sc_ref, rw_ref, hp_ref, lg_ref):
    h = _norm_mod(x_ref[...], g_ref[...], sh_ref[...], sc_ref[...])
    lg_ref[...] = lax.dot_general(rw_ref[...], h, (((1,), (1,)), ((), ())),
                                  precision=lax.Precision.HIGHEST, preferred_element_type=F32)
    half = D_MODEL // 2
    hp_ref[...] = pltpu.pack_elementwise([h[:, :half], h[:, half:]], packed_dtype=BF16)


def _norm2(x, norm_g4, mod5, router_wt, layer):
    n = x.shape[0]
    nlt, tpb = N_LAT // TM, SEQ // TM
    return pl.pallas_call(
        _norm2_kernel,
        grid=(n // TM,),
        in_specs=[
            pl.BlockSpec((TM, D_MODEL), lambda i: (i, 0)),
            pl.BlockSpec((None, None, 1, D_MODEL), lambda i: (layer, 1, 0, 0)),
            _mod_spec(layer, 3, nlt, tpb, 1),
            _mod_spec(layer, 4, nlt, tpb, 1),
            pl.BlockSpec((N_EXPERTS, D_MODEL), lambda i: (0, 0)),
        ],
        out_specs=[
            pl.BlockSpec((TM, D_MODEL // 2), lambda i: (i, 0)),
            pl.BlockSpec((N_EXPERTS, TM), lambda i: (0, i)),
        ],
        out_shape=[
            jax.ShapeDtypeStruct((n, D_MODEL // 2), jnp.uint32),
            jax.ShapeDtypeStruct((N_EXPERTS, n), F32),
        ],
        compiler_params=_cparams(("arbitrary",)),
        name="norm2_router_logits",
    )(x, norm_g4, mod5, mod5, router_wt)


def _mm_kernel(*refs, nk, n_extra, epilogue):
    a_ref, w_ref = refs[0], refs[1]
    extra = refs[2:2 + n_extra]
    o_ref = refs[2 + n_extra]
    part = jnp.dot(a_ref[...].astype(BF16), w_ref[...].astype(BF16), preferred_element_type=F32)
    if nk == 1:
        epilogue(part, o_ref, *extra)
        return
    acc_ref = refs[3 + n_extra]
    k = pl.program_id(2)

    @pl.when(k == 0)
    def _():
        acc_ref[...] = part

    @pl.when(k > 0)
    def _():
        acc_ref[...] += part

    @pl.when(k == nk - 1)
    def _():
        epilogue(acc_ref[...], o_ref, *extra)


def _matmul(a, a_spec, w, w_spec, extras, out_shape, out_spec, grid, epilogue, tm, tn, name):
    nk = grid[2]
    kern = functools.partial(_mm_kernel, nk=nk, n_extra=len(extras), epilogue=epilogue)
    scratch = [pltpu.VMEM((tm, tn), F32)] if nk > 1 else []
    return pl.pallas_call(
        kern,
        grid=grid,
        in_specs=[a_spec, w_spec] + [s for _, s in extras],
        out_specs=out_spec,
        out_shape=out_shape,
        scratch_shapes=scratch,
        compiler_params=_cparams(("arbitrary", "arbitrary", "arbitrary")),
        name=name,
    )(a, w, *[x for x, _ in extras])


def _epi_plain(acc, o_ref):
    o_ref[...] = acc.astype(o_ref.dtype)


def _epi_resid_gate(acc, o_ref, x_ref, gate_ref):
    o_ref[...] = x_ref[...] + gate_ref[...] * acc


def _epi_rmsnorm(acc, o_ref, g_ref):
    y = acc * lax.rsqrt(jnp.mean(acc * acc, axis=-1, keepdims=True) + NORM_EPS) * g_ref[...]
    o_ref[...] = y.astype(o_ref.dtype)


def _rope_rotate(acc, cos_ref, sin_ref):
    tn = acc.shape[1]
    reps = tn // LANES
    cos = jnp.concatenate([cos_ref[...]] * reps, axis=1) if reps > 1 else cos_ref[...]
    sin = jnp.concatenate([sin_ref[...]] * reps, axis=1) if reps > 1 else sin_ref[...]
    up = pltpu.roll(acc, tn - 16, 1)
    dn = pltpu.roll(acc, 16, 1)
    lane = lax.broadcasted_iota(jnp.int32, acc.shape, 1)
    rot = jnp.where((lane % 32) < 16, up, dn)
    return acc * cos + rot * sin


def _epi_rope(acc, o_ref, cos_ref, sin_ref, *, rope_cols, scale_cols, scale, n_cols):
    tn = acc.shape[1]
    j = pl.program_id(0)

    def finish(r):
        if scale_cols:
            col = j * tn + lax.broadcasted_iota(jnp.int32, acc.shape, 1)
            r = jnp.where(col < scale_cols, r * scale, r)
        o_ref[...] = r.astype(o_ref.dtype)

    def roped():
        col = j * tn + lax.broadcasted_iota(jnp.int32, acc.shape, 1)
        r = _rope_rotate(acc, cos_ref, sin_ref)
        finish(jnp.where((col >= rope_cols[0]) & (col < rope_cols[1]), r, acc))

    if rope_cols[0] < tn and rope_cols[1] > n_cols - tn:
        roped()
        return
    touches_rope = (j * tn < rope_cols[1]) & ((j + 1) * tn > rope_cols[0])
    pl.when(touches_rope)(roped)
    pl.when(jnp.logical_not(touches_rope))(lambda: finish(acc))


def _rope_tables(tm):
    d_axis = A_DH // 2
    inv_freq = ROPE_THETA ** (-jnp.arange(0, d_axis, 2, dtype=F32) / d_axis)
    t = jnp.arange(SEQ, dtype=jnp.int32)
    ang_r = (t // GRID_W).astype(F32)[:, None] * inv_freq
    ang_c = (t % GRID_W).astype(F32)[:, None] * inv_freq
    ang = jnp.concatenate([ang_r, ang_r, ang_c, ang_c] * 2, axis=1)
    sign = jnp.tile(jnp.concatenate([-jnp.ones(16, F32), jnp.ones(16, F32)]), 4)
    cos = jnp.concatenate([jnp.cos(ang), jnp.ones((tm, LANES), F32)], axis=0)
    sin = jnp.concatenate([jnp.sin(ang) * sign, jnp.zeros((tm, LANES), F32)], axis=0)
    return cos, sin


def _rope_spec(tm):
    nlt, tpb = N_LAT // tm, SEQ // tm
    return pl.BlockSpec((tm, LANES), lambda j, i, k: (jnp.where(i < nlt, i % tpb, tpb), 0))


def _proj(a, a_col, w, layer, n_out, tn, epilogue, extras, out_dtype, name):
    n = a.shape[0]
    kdim = w.shape[-2]
    grid = (n_out // tn, pl.cdiv(n, TM_PROJ), 1)
    if w.ndim == 3:
        w_spec = pl.BlockSpec((None, kdim, tn), lambda j, i, k: (layer, 0, j))
    else:
        w_spec = pl.BlockSpec((kdim, tn), lambda j, i, k: (0, j))
    return _matmul(
        a, pl.BlockSpec((TM_PROJ, kdim), lambda j, i, k: (i, a_col)), w, w_spec, extras,
        jax.ShapeDtypeStruct((n, n_out), out_dtype), pl.BlockSpec((TM_PROJ, tn), lambda j, i, k: (i, j)),
        grid, epilogue, TM_PROJ, tn, name)


def _out_proj_resid(o, w3, layer_w, x, mod5, layer, name):
    nlt, tpb = N_LAT // TM_PROJ, SEQ // TM_PROJ
    tn = 1024
    gate_spec = pl.BlockSpec(
        (None, None, None, 1, tn),
        lambda j, i, k: (layer, _row_group(i, nlt, tpb), 2, 0, j))
    extras = [(x, pl.BlockSpec((TM_PROJ, tn), lambda j, i, k: (i, j))), (mod5, gate_spec)]
    return _proj(o, 0, w3, layer_w, D_MODEL, tn, _epi_resid_gate, extras, F32, name)


def _dup_half(t, upper):
    h = t[:, A_DH:] if upper else t[:, :A_DH]
    return jnp.concatenate([h, h], axis=1)


def _gqa_kernel(sink_ref, q_ref, kp_ref, ks_ref, kn_ref, vp_ref, vs_ref, vn_ref, kc_ref, vc_ref, o_ref,
                *, n_lat_blocks, blocks_per_seq):
    n = pl.program_id(0)
    is_lat = n < n_lat_blocks
    nb = n % blocks_per_seq
    iq = lax.broadcasted_iota(jnp.int32, (BLK, 3 * BLK), 0)
    ik = lax.broadcasted_iota(jnp.int32, (BLK, 3 * BLK), 1)
    mask = (ik >= iq) & (ik <= iq + 2 * BLK)
    mask = mask & ((nb > 0) | (ik >= BLK)) & ((nb < blocks_per_seq - 1) | (ik < 2 * BLK)) & is_lat
    lane = lax.broadcasted_iota(jnp.int32, (BLK, LANES), 1)
    lo_half = lane < A_DH
    keep_lo = lo_half.astype(F32).astype(BF16)
    keep_hi = (1.0 - lo_half.astype(F32)).astype(BF16)
    n_pairs = A_GROUPS // 2
    for hk in range(A_HKV):
        c0 = (hk // 2) * LANES
        upper = (hk % 2) == 1
        k_all = jnp.concatenate([
            _dup_half(kp_ref[:, c0:c0 + LANES], upper), _dup_half(ks_ref[:, c0:c0 + LANES], upper),
            _dup_half(kn_ref[:, c0:c0 + LANES], upper), _dup_half(kc_ref[:, c0:c0 + LANES], upper)], axis=0)
        v_all = jnp.concatenate([
            _dup_half(vp_ref[:, c0:c0 + LANES], upper), _dup_half(vs_ref[:, c0:c0 + LANES], upper),
            _dup_half(vn_ref[:, c0:c0 + LANES], upper), _dup_half(vc_ref[:, c0:c0 + LANES], upper)], axis=0)
        ones = jnp.ones(v_all.shape, BF16)
        v_lo = v_all * keep_lo[0:1] + ones * keep_hi[0:1]
        v_hi = v_all * keep_hi[0:1] + ones * keep_lo[0:1]
        lhs = []
        for p in range(n_pairs):
            qp = q_ref[:, hk * A_GROUPS * A_DH + p * LANES: hk * A_GROUPS * A_DH + (p + 1) * LANES]
            lhs.append(qp * keep_lo)
            lhs.append(qp * keep_hi)
        lhs = jnp.concatenate(lhs, axis=0)
        s = lax.dot_general(lhs, k_all, (((1,), (1,)), ((), ())), preferred_element_type=F32)
        probs, sink_terms = [], []
        for g in range(A_GROUPS):
            sg = s[g * BLK:(g + 1) * BLK]
            s_loc = jnp.where(mask, sg[:, :3 * BLK], NEG_INF)
            s_ctx = sg[:, 3 * BLK:]
            sink = sink_ref[hk * A_GROUPS + g]
            m = jnp.maximum(jnp.maximum(jnp.max(s_loc, axis=-1, keepdims=True),
                                        jnp.max(s_ctx, axis=-1, keepdims=True)), sink)
            sink_terms.append(jnp.exp(sink - m))
            probs.append(jnp.exp(jnp.concatenate([s_loc - m, s_ctx - m], axis=1).astype(BF16)))
        res_lo = jnp.dot(jnp.concatenate(probs[0::2], axis=0), v_lo, preferred_element_type=F32)
        res_hi = jnp.dot(jnp.concatenate(probs[1::2], axis=0), v_hi, preferred_element_type=F32)
        for p in range(n_pairs):
            r_lo = res_lo[p * BLK:(p + 1) * BLK]
            r_hi = res_hi[p * BLK:(p + 1) * BLK]
            o_lo = r_lo / (r_lo[:, A_DH:A_DH + 1] + sink_terms[2 * p])
            o_hi = r_hi / (r_hi[:, 0:1] + sink_terms[2 * p + 1])
            col = hk * A_GROUPS * A_DH + p * LANES
            o_ref[:, col:col + LANES] = jnp.where(lo_half, o_lo, o_hi).astype(o_ref.dtype)


def _gqa_attention(qkv, sink, need_ctx):
    bps = SEQ // BLK
    nlb = BATCH * bps
    n_ctx_blocks = N_CTX // BLK if need_ctx else 0
    nq = nlb + n_ctx_blocks
    kcol, vcol = (A_HQ * A_DH) // 256, (A_HQ * A_DH) // 256 + 1
    cpb = CTX_LEN // BLK

    def batch_of(n):
        return jnp.where(n < nlb, n // bps, (n - nlb) // cpb)

    def loc_map(delta, col):
        def imap(n, s):
            nb = n % bps
            blk = jnp.clip(nb + delta, 0, bps - 1)
            return (jnp.where(n < nlb, (n // bps) * bps + blk, 0), col)
        return imap

    def ctx_map(col):
        return lambda n, s: (N_LAT // CTX_LEN + batch_of(n), col)

    kv_blk = (BLK, 256)
    grid_spec = pltpu.PrefetchScalarGridSpec(
        num_scalar_prefetch=1,
        grid=(nq,),
        in_specs=[
            pl.BlockSpec((BLK, A_HQ * A_DH), lambda n, s: (n, 0)),
            pl.BlockSpec(kv_blk, loc_map(-1, kcol)), pl.BlockSpec(kv_blk, loc_map(0, kcol)),
            pl.BlockSpec(kv_blk, loc_map(1, kcol)),
            pl.BlockSpec(kv_blk, loc_map(-1, vcol)), pl.BlockSpec(kv_blk, loc_map(0, vcol)),
            pl.BlockSpec(kv_blk, loc_map(1, vcol)),
            pl.BlockSpec((CTX_LEN, 256), ctx_map(kcol)), pl.BlockSpec((CTX_LEN, 256), ctx_map(vcol)),
        ],
        out_specs=pl.BlockSpec((BLK, A_HQ * A_DH), lambda n, s: (n, 0)),
    )
    kern = functools.partial(_gqa_kernel, n_lat_blocks=nlb, blocks_per_seq=bps)
    return pl.pallas_call(
        kern,
        grid_spec=grid_spec,
        out_shape=jax.ShapeDtypeStruct((nq * BLK, A_HQ * A_DH), BF16),
        compiler_params=_cparams(("arbitrary",)),
        name="gqa_window_attn",
    )(sink, qkv, qkv, qkv, qkv, qkv, qkv, qkv, qkv, qkv)


MLA_TQ = 1024
MLA_TK = 1024


def _mla_kernel(*refs, n_lat_keys):
    if n_lat_keys:
        (qn_ref, qr_ref, kn_ref, v_ref, kr_ref, knc_ref, vc_ref, krc_ref, o_ref, kcat, vcat) = refs
    else:
        (qn_ref, qr_ref, knc_ref, vc_ref, krc_ref, o_ref, kcat, vcat) = refs
    qi = pl.program_id(2)
    nk = n_lat_keys + CTX_LEN

    @pl.when(qi == 0)
    def _():
        if n_lat_keys:
            kcat[0:n_lat_keys, 0:LANES] = kn_ref[...]
            kcat[0:n_lat_keys, LANES:2 * LANES] = kr_ref[...]
            vcat[0:n_lat_keys, 0:M_V] = v_ref[...]
        kcat[n_lat_keys:nk, 0:LANES] = knc_ref[...]
        kcat[n_lat_keys:nk, LANES:2 * LANES] = krc_ref[...]
        vcat[n_lat_keys:nk, 0:M_V] = vc_ref[...]
        vcat[:, M_V:M_V + LANES] = jnp.ones((nk, LANES), BF16)

    q = jnp.concatenate([qn_ref[...], qr_ref[...]], axis=1)

    def chunk(k0, size):
        kc = kcat[k0:k0 + size, :]
        vc = vcat[k0:k0 + size, :]
        s = lax.dot_general(q, kc, (((1,), (1,)), ((), ())), preferred_element_type=F32)
        m_c = jnp.max(s, axis=-1, keepdims=True)
        p = jnp.exp((s - m_c).astype(BF16))
        ol = jnp.dot(p, vc, preferred_element_type=F32)
        return m_c, ol[:, M_V:M_V + 1], ol[:, :M_V]

    parts = [chunk(c * MLA_TK, MLA_TK) for c in range(n_lat_keys // MLA_TK)]
    parts.append(chunk(n_lat_keys, CTX_LEN))
    m = parts[0][0]
    for m_c, _, _ in parts[1:]:
        m = jnp.maximum(m, m_c)
    l = None
    acc = None
    for m_c, l_c, o_c in parts:
        w_c = jnp.exp(m_c - m)
        l = w_c * l_c if l is None else l + w_c * l_c
        acc = w_c * o_c if acc is None else acc + w_c * o_c
    o_ref[...] = (acc / l).astype(o_ref.dtype)


def _mla_attention(qcat, kv, kr, latent):
    hq = M_H
    ctx_row_blk = N_LAT // CTX_LEN
    if latent:
        tq, nqt, n_lat_keys = MLA_TQ, SEQ // MLA_TQ, SEQ
        row0 = lambda b, qi: b * nqt + qi
        n_out = N_LAT
    else:
        tq, nqt, n_lat_keys = CTX_LEN, 1, 0
        row0 = lambda b, qi: ctx_row_blk + b
        n_out = N_CTX
    in_specs = [
        pl.BlockSpec((tq, LANES), lambda b, h, qi: (row0(b, qi), h)),
        pl.BlockSpec((tq, LANES), lambda b, h, qi: (row0(b, qi), hq + h)),
    ]
    args = [qcat, qcat]
    if latent:
        in_specs += [
            pl.BlockSpec((SEQ, LANES), lambda b, h, qi: (b, 2 * h)),
            pl.BlockSpec((SEQ, LANES), lambda b, h, qi: (b, 2 * h + 1)),
            pl.BlockSpec((SEQ, LANES), lambda b, h, qi: (b, 0)),
        ]
        args += [kv, kv, kr]
    in_specs += [
        pl.BlockSpec((CTX_LEN, LANES), lambda b, h, qi: (ctx_row_blk + b, 2 * h)),
        pl.BlockSpec((CTX_LEN, LANES), lambda b, h, qi: (ctx_row_blk + b, 2 * h + 1)),
        pl.BlockSpec((CTX_LEN, LANES), lambda b, h, qi: (ctx_row_blk + b, 0)),
    ]
    args += [kv, kv, kr]
    if latent:
        out_spec = pl.BlockSpec((tq, LANES), lambda b, h, qi: (b * nqt + qi, h))
    else:
        out_spec = pl.BlockSpec((tq, LANES), lambda b, h, qi: (b, h))
    nk = n_lat_keys + CTX_LEN
    return pl.pallas_call(
        functools.partial(_mla_kernel, n_lat_keys=n_lat_keys),
        grid=(BATCH, hq, nqt),
        in_specs=in_specs,
        out_specs=out_spec,
        out_shape=jax.ShapeDtypeStruct((n_out, hq * M_V), BF16),
        scratch_shapes=[pltpu.VMEM((nk, 2 * LANES), BF16), pltpu.VMEM((nk, M_V + LANES), BF16)],
        compiler_params=_cparams(("arbitrary", "arbitrary", "arbitrary")),
        name="mla_attn_lat" if latent else "mla_attn_ctx",
    )(*args)


def _dft_mats(n, scale):
    sub = 64
    k = jnp.arange(n, dtype=jnp.int32)
    j1 = jnp.arange(n // sub, dtype=jnp.int32) * sub
    j0 = jnp.arange(sub, dtype=jnp.int32)
    a = ((j1[:, None] * k[None, :]) % n).astype(F32) * (2.0 * math.pi / n)
    b = ((j0[:, None] * k[None, :]) % n).astype(F32) * (2.0 * math.pi / n)
    ca, sa = (jnp.cos(a) * scale)[:, None, :], (jnp.sin(a) * scale)[:, None, :]
    cb, sb = jnp.cos(b)[None, :, :], jnp.sin(b)[None, :, :]
    return (ca * cb - sa * sb).reshape(n, n), (sa * cb + ca * sb).reshape(n, n)


def _fnet_stage1_kernel(h_ref, w_ref, z_ref):
    z = jnp.dot(h_ref[...], w_ref[...], preferred_element_type=F32)
    z_ref[0] = z[:, :F_GROUP_DIM].astype(z_ref.dtype)
    z_ref[1] = z[:, F_GROUP_DIM:].astype(z_ref.dtype)


def _fnet_stage1(h1, w1, row_blk0, seq_len, nbatch):
    tm = min(TM, seq_len)
    tiles = seq_len // tm
    return pl.pallas_call(
        _fnet_stage1_kernel,
        grid=(nbatch, tiles, F_GROUPS),
        in_specs=[
            pl.BlockSpec((tm, F_GROUP_DIM), lambda b, i, g: (row_blk0 + b * tiles + i, g)),
            pl.BlockSpec((F_GROUP_DIM, 2 * F_GROUP_DIM), lambda b, i, g: (0, 0)),
        ],
        out_specs=pl.BlockSpec((None, 2, tm, F_GROUP_DIM), lambda b, i, g: (b, 0, i, g)),
        out_shape=jax.ShapeDtypeStruct((nbatch, 2, seq_len, D_MODEL), BF16),
        compiler_params=_cparams(("arbitrary", "arbitrary", "arbitrary")),
        name="fnet_channel_dft",
    )(h1, w1)


def _fnet_stage2_kernel(a_ref, z_ref, o_ref, acc_ref, *, nk):
    k = pl.program_id(2)
    part = jnp.dot(a_ref[...], z_ref[...], preferred_element_type=F32)

    @pl.when(k == 0)
    def _():
        acc_ref[...] = part

    @pl.when(k > 0)
    def _():
        acc_ref[...] += part

    @pl.when(k == nk - 1)
    def _():
        o_ref[...] = acc_ref[...].astype(o_ref.dtype)


def _fnet_stage2(a2, z, seq_len, nbatch):
    tm = min(1024, seq_len)
    tk = min(1024, seq_len)
    nk = 2 * seq_len // tk
    z2 = z.reshape(nbatch, 2 * seq_len, D_MODEL)
    return pl.pallas_call(
        functools.partial(_fnet_stage2_kernel, nk=nk),
        grid=(nbatch, seq_len // tm, nk),
        in_specs=[
            pl.BlockSpec((tm, tk), lambda b, i, k: (i, k)),
            pl.BlockSpec((None, tk, D_MODEL), lambda b, i, k: (b, k, 0)),
        ],
        out_specs=pl.BlockSpec((None, tm, D_MODEL), lambda b, i, k: (b, i, 0)),
        out_shape=jax.ShapeDtypeStruct((nbatch, seq_len, D_MODEL), BF16),
        scratch_shapes=[pltpu.VMEM((tm, D_MODEL), F32)],
        compiler_params=_cparams(("arbitrary", "arbitrary", "arbitrary")),
        name="fnet_position_dft",
    )(a2, z2)


ROUTER_TILE = 512


def _top2_rows(v, rowf):
    big = float(v.shape[0])
    m1 = jnp.max(v, axis=0, keepdims=True)
    i1 = jnp.min(jnp.where(v == m1, rowf, big), axis=0, keepdims=True)
    v2 = jnp.where(rowf == i1, NEG_INF, v)
    m2 = jnp.max(v2, axis=0, keepdims=True)
    i2 = jnp.min(jnp.where(v2 == m2, rowf, big), axis=0, keepdims=True)
    return m1, i1, m2, i2


def _router_kernel(lg_ref, b_ref, idx_ref, w_ref, cnt_ref, carry_ref):
    step = pl.program_id(0)
    t = lg_ref.shape[1]

    @pl.when(step == 0)
    def _():
        carry_ref[...] = jnp.zeros_like(carry_ref)

    scores = jax.nn.sigmoid(lg_ref[...])
    sel = scores + b_ref[...][:, 0:1]
    row = lax.broadcasted_iota(jnp.int32, (N_EXPERTS, t), 0)
    rowf = row.astype(F32)
    grp = row // EXPERTS_PER_GROUP
    best = jnp.zeros((1, t), jnp.int32)
    best_v = None
    for g in range(N_EXPERT_GROUPS):
        m1, _, m2, _ = _top2_rows(jnp.where(grp == g, sel, NEG_INF), rowf)
        gs = m1 + m2
        if g == 0:
            best_v = gs
        else:
            upd = gs > best_v
            best = jnp.where(upd, g, best)
            best_v = jnp.where(upd, gs, best_v)
    _, e0, _, e1 = _top2_rows(jnp.where(grp == best, sel, NEG_INF), rowf)
    oh0 = rowf == e0
    oh1 = rowf == e1
    s0 = jnp.sum(jnp.where(oh0, scores, 0.0), axis=0, keepdims=True)
    s1 = jnp.sum(jnp.where(oh1, scores, 0.0), axis=0, keepdims=True)
    tot = s0 + s1
    cnt = jnp.where(oh0 | oh1, 1.0, 0.0)
    a = lax.broadcasted_iota(jnp.int32, (t, t), 0)
    bcol = lax.broadcasted_iota(jnp.int32, (t, t), 1)
    tri = jnp.where(a < bcol, 1.0, 0.0).astype(BF16)
    excl = jnp.dot(cnt.astype(BF16), tri, preferred_element_type=F32) + carry_ref[...][:, 0:1]
    r0 = jnp.sum(jnp.where(oh0, excl, 0.0), axis=0, keepdims=True)
    r1 = jnp.sum(jnp.where(oh1, excl, 0.0), axis=0, keepdims=True)
    new_carry = carry_ref[...] + jnp.sum(cnt, axis=1, keepdims=True)
    carry_ref[...] = new_carry
    idx_ref[...] = jnp.zeros(idx_ref.shape, jnp.int32)
    idx_ref[0:1, :] = e0.astype(jnp.int32)
    idx_ref[1:2, :] = e1.astype(jnp.int32)
    idx_ref[2:3, :] = r0.astype(jnp.int32)
    idx_ref[3:4, :] = r1.astype(jnp.int32)
    w_ref[...] = jnp.zeros(w_ref.shape, F32)
    w_ref[0:1, :] = s0 / tot
    w_ref[1:2, :] = s1 / tot
    cnt_ref[...] = new_carry.astype(jnp.int32)


def _router(logits_t, router_b):
    n = logits_t.shape[1]
    t = ROUTER_TILE
    b2 = jnp.broadcast_to(router_b.astype(F32)[:, None], (N_EXPERTS, LANES))
    return pl.pallas_call(
        _router_kernel,
        grid=(n // t,),
        in_specs=[pl.BlockSpec((N_EXPERTS, t), lambda i: (0, i)),
                  pl.BlockSpec((N_EXPERTS, LANES), lambda i: (0, 0))],
        out_specs=[pl.BlockSpec((8, t), lambda i: (0, i)),
                   pl.BlockSpec((8, t), lambda i: (0, i)),
                   pl.BlockSpec((N_EXPERTS, LANES), lambda i: (0, 0))],
        out_shape=[jax.ShapeDtypeStruct((8, n), jnp.int32),
                   jax.ShapeDtypeStruct((8, n), F32),
                   jax.ShapeDtypeStruct((N_EXPERTS, LANES), jnp.int32)],
        scratch_shapes=[pltpu.VMEM((N_EXPERTS, LANES), F32)],
        compiler_params=_cparams(("arbitrary",)),
        name="moe_router",
    )(logits_t, b2)


def _plan_items(counts):
    n_chunks = (counts + ITEM_ROWS - 1) // ITEM_ROWS
    ends = jnp.cumsum(n_chunks)
    starts = ends - n_chunks
    base = starts * ITEM_ROWS
    n_valid = ends[-1]
    item = jnp.arange(N_ITEMS, dtype=jnp.int32)
    item_src = jnp.minimum(item, n_valid - 1)
    expert = jnp.sum((ends[None, :] <= item_src[:, None]).astype(jnp.int32), axis=1)
    rows = jnp.clip(counts[expert] - (item_src - starts[expert]) * ITEM_ROWS, 0, ITEM_ROWS)
    rows = jnp.where(item < n_valid, rows, 0).astype(jnp.int32)
    nblk = (rows + ROW_BLK - 1) // ROW_BLK
    return base.astype(jnp.int32), item_src.astype(jnp.int32), expert, nblk, rows


DISPATCH_TILE = 512


def _dispatch_kernel(p0_ref, p1_ref, hp_ref, hs_ref, sem):
    base = pl.program_id(0) * DISPATCH_TILE

    def copies(r):
        t = base + r
        return (pltpu.make_async_copy(hp_ref.at[pl.ds(r, 1)], hs_ref.at[pl.ds(p0_ref[t], 1)], sem),
                pltpu.make_async_copy(hp_ref.at[pl.ds(r, 1)], hs_ref.at[pl.ds(p1_ref[t], 1)], sem))

    def start(r, c):
        a, b = copies(r)
        a.start(priority=0)
        b.start(priority=1)
        return c

    def wait(r, c):
        a, b = copies(r)
        a.wait()
        b.wait()
        return c

    lax.fori_loop(0, DISPATCH_TILE, start, 0, unroll=8)
    lax.fori_loop(0, DISPATCH_TILE, wait, 0, unroll=8)


def _dispatch(hp, pos0, pos1):
    n = hp.shape[0]
    grid_spec = pltpu.PrefetchScalarGridSpec(
        num_scalar_prefetch=2,
        grid=(n // DISPATCH_TILE,),
        in_specs=[pl.BlockSpec((DISPATCH_TILE, D_MODEL // 2), lambda i, p0, p1: (i, 0))],
        out_specs=pl.BlockSpec(memory_space=pl.ANY),
        scratch_shapes=[pltpu.SemaphoreType.DMA(())],
    )
    return pl.pallas_call(
        _dispatch_kernel,
        grid_spec=grid_spec,
        out_shape=jax.ShapeDtypeStruct((N_ITEMS * ITEM_ROWS, D_MODEL // 2), hp.dtype),
        compiler_params=pltpu.CompilerParams(dimension_semantics=("arbitrary",), has_side_effects=True),
        name="moe_dispatch",
    )(pos0, pos1, hp)


def _gate_up_kernel(src_ref, exp_ref, nblk_ref, rows_ref, hs_ref, wg_ref, wu_ref, a_ref, hb_ref):
    del src_ref, exp_ref
    i = pl.program_id(0)
    f = pl.program_id(1)
    nblk = nblk_ref[i]
    n_rows = rows_ref[i]
    half = D_MODEL // 2

    @pl.when(nblk > 0)
    def _():
        @pl.when(f == 0)
        def _():
            def unpack(r, c):
                rows = pl.ds(pl.multiple_of(r * ROW_BLK, ROW_BLK), ROW_BLK)
                u = hs_ref[rows, :]
                live = (r * ROW_BLK + lax.broadcasted_iota(jnp.int32, (ROW_BLK, half), 0)) < n_rows
                lo = pltpu.unpack_elementwise(u, index=0, packed_dtype=BF16, unpacked_dtype=F32)
                hi = pltpu.unpack_elementwise(u, index=1, packed_dtype=BF16, unpacked_dtype=F32)
                hb_ref[rows, 0:half] = jnp.where(live, lo, 0.0).astype(BF16)
                hb_ref[rows, half:D_MODEL] = jnp.where(live, hi, 0.0).astype(BF16)
                return c
            lax.fori_loop(0, nblk, unpack, 0)

        for m in range(1, ITEM_ROWS // ROW_BLK + 1):
            rows = m * ROW_BLK

            @pl.when(nblk == m)
            def _():
                w = jnp.concatenate([wg_ref[...].astype(BF16), wu_ref[...].astype(BF16)], axis=1)
                res = jnp.dot(hb_ref[0:rows, :], w, preferred_element_type=F32)
                g = res[:, :FF_TILE]
                u = res[:, FF_TILE:]
                a_ref[0:rows, :] = (g * jax.nn.sigmoid(g) * u).astype(a_ref.dtype)
                if rows < ITEM_ROWS:
                    a_ref[rows:ITEM_ROWS, :] = jnp.zeros((ITEM_ROWS - rows, FF_TILE), a_ref.dtype)


def _gate_up(hs, w_gate, w_up, layer, item_src, expert, nblk, rows):
    nf = D_FF // FF_TILE

    def frozen_f(i, f, nb):
        return jnp.where(nb[i] > 0, f, nf - 1)

    grid_spec = pltpu.PrefetchScalarGridSpec(
        num_scalar_prefetch=4,
        grid=(N_ITEMS, nf),
        in_specs=[
            pl.BlockSpec((ITEM_ROWS, D_MODEL // 2), lambda i, f, src, ex, nb, rw: (src[i], 0)),
            pl.BlockSpec((None, None, D_MODEL, FF_TILE),
                         lambda i, f, src, ex, nb, rw: (layer, ex[i], 0, frozen_f(i, f, nb))),
            pl.BlockSpec((None, None, D_MODEL, FF_TILE),
                         lambda i, f, src, ex, nb, rw: (layer, ex[i], 0, frozen_f(i, f, nb))),
        ],
        out_specs=pl.BlockSpec((ITEM_ROWS, FF_TILE),
                               lambda i, f, src, ex, nb, rw: (src[i], frozen_f(i, f, nb))),
        scratch_shapes=[pltpu.VMEM((ITEM_ROWS, D_MODEL), BF16)],
    )
    return pl.pallas_call(
        _gate_up_kernel,
        grid_spec=grid_spec,
        out_shape=jax.ShapeDtypeStruct((N_ITEMS * ITEM_ROWS, D_FF), BF16),
        compiler_params=_cparams(("arbitrary", "arbitrary")),
        name="moe_gate_up",
    )(item_src, expert, nblk, rows, hs, w_gate, w_up)


def _down_kernel(src_ref, exp_ref, nblk_ref, a_ref, wlo_ref, whi_ref, y_ref):
    del src_ref, exp_ref
    i = pl.program_id(0)
    nblk = nblk_ref[i]

    for m in range(1, ITEM_ROWS // ROW_BLK + 1):
        rows = m * ROW_BLK

        @pl.when(nblk == m)
        def _():
            w = jnp.concatenate([wlo_ref[...].astype(BF16), whi_ref[...].astype(BF16)], axis=1)
            res = jnp.dot(a_ref[0:rows, :], w, preferred_element_type=F32)
            y_ref[0:rows, :] = pltpu.pack_elementwise([res[:, :DOWN_TILE], res[:, DOWN_TILE:]], packed_dtype=BF16)
            if rows < ITEM_ROWS:
                zero = jnp.zeros((ITEM_ROWS - rows, DOWN_TILE), F32)
                y_ref[rows:ITEM_ROWS, :] = pltpu.pack_elementwise([zero, zero], packed_dtype=BF16)


def _down(a, w_down, layer, item_src, expert, nblk):
    nj = D_MODEL // (2 * DOWN_TILE)

    def frozen_j(i, j, nb):
        return jnp.where(nb[i] > 0, j, nj - 1)

    grid_spec = pltpu.PrefetchScalarGridSpec(
        num_scalar_prefetch=3,
        grid=(N_ITEMS, nj),
        in_specs=[
            pl.BlockSpec((ITEM_ROWS, D_FF), lambda i, j, src, ex, nb: (src[i], 0)),
            pl.BlockSpec((None, None, D_FF, DOWN_TILE),
                         lambda i, j, src, ex, nb: (layer, ex[i], 0, frozen_j(i, j, nb))),
            pl.BlockSpec((None, None, D_FF, DOWN_TILE),
                         lambda i, j, src, ex, nb: (layer, ex[i], 0, nj + frozen_j(i, j, nb))),
        ],
        out_specs=pl.BlockSpec((ITEM_ROWS, DOWN_TILE), lambda i, j, src, ex, nb: (src[i], frozen_j(i, j, nb))),
    )
    return pl.pallas_call(
        _down_kernel,
        grid_spec=grid_spec,
        out_shape=jax.ShapeDtypeStruct((N_ITEMS * ITEM_ROWS, D_MODEL // 2), jnp.uint32),
        compiler_params=_cparams(("arbitrary", "arbitrary")),
        name="moe_down",
    )(item_src, expert, nblk, a, w_down, w_down)


COMBINE_TILE = 512


def _combine_kernel(p0_ref, p1_ref, x_ref, gate_ref, w0_ref, w1_ref, g_ref, sh_ref, sc_ref, ys_ref, *rest, final):
    if final:
        o_ref, y0_ref, y1_ref, sem = rest
    else:
        xo_ref, h_ref, y0_ref, y1_ref, sem = rest
    base = pl.program_id(0) * COMBINE_TILE

    def copies(r):
        t = base + r
        return (pltpu.make_async_copy(ys_ref.at[pl.ds(p0_ref[t], 1)], y0_ref.at[pl.ds(r, 1)], sem),
                pltpu.make_async_copy(ys_ref.at[pl.ds(p1_ref[t], 1)], y1_ref.at[pl.ds(r, 1)], sem))

    def start(r, c):
        a, b = copies(r)
        a.start(priority=0)
        b.start(priority=1)
        return c

    def wait(r, c):
        a, b = copies(r)
        a.wait()
        b.wait()
        return c

    lax.fori_loop(0, COMBINE_TILE, start, 0, unroll=8)
    lax.fori_loop(0, COMBINE_TILE, wait, 0, unroll=8)
    half = D_MODEL // 2
    w0 = w0_ref[...]
    w1 = w1_ref[...]
    xs = []
    for hh in range(2):
        cols = slice(hh * half, (hh + 1) * half)
        y = (w0 * pltpu.unpack_elementwise(y0_ref[...], index=hh, packed_dtype=BF16, unpacked_dtype=F32)
             + w1 * pltpu.unpack_elementwise(y1_ref[...], index=hh, packed_dtype=BF16, unpacked_dtype=F32))
        xs.append(x_ref[:, cols] + gate_ref[:, cols] * y)
    ssq = jnp.sum(xs[0] * xs[0], axis=-1, keepdims=True) + jnp.sum(xs[1] * xs[1], axis=-1, keepdims=True)
    inv = lax.rsqrt(ssq * (1.0 / D_MODEL) + NORM_EPS)
    for hh in range(2):
        cols = slice(hh * half, (hh + 1) * half)
        normed = xs[hh] * inv * g_ref[:, cols]
        if final:
            o_ref[:, cols] = normed
        else:
            xo_ref[:, cols] = xs[hh]
            h_ref[:, cols] = (normed * (1.0 + sc_ref[:, cols]) + sh_ref[:, cols]).astype(h_ref.dtype)


def _combine(x, ys, pos0, pos1, w0, w1, mod5, layer, norm_g, next_layer):
    n = x.shape[0]
    t = COMBINE_TILE
    nlt, tpb = N_LAT // t, SEQ // t
    final = next_layer is None
    mod_layer = layer if final else next_layer
    g_layer = 0 if final else next_layer

    def mod_spec(lyr, which):
        return pl.BlockSpec((None, None, None, 1, D_MODEL),
                            lambda i, p0, p1: (lyr, _row_group(i, nlt, tpb), which, 0, 0))

    row_spec = pl.BlockSpec((t, D_MODEL), lambda i, p0, p1: (i, 0))
    grid_spec = pltpu.PrefetchScalarGridSpec(
        num_scalar_prefetch=2,
        grid=(n // t,),
        in_specs=[
            row_spec,
            mod_spec(layer, 5),
            pl.BlockSpec((t, 1), lambda i, p0, p1: (i, 0)),
            pl.BlockSpec((t, 1), lambda i, p0, p1: (i, 0)),
            pl.BlockSpec((None, None, 1, D_MODEL), lambda i, p0, p1: (g_layer, 0, 0, 0)),
            mod_spec(mod_layer, 0),
            mod_spec(mod_layer, 1),
            pl.BlockSpec(memory_space=pl.ANY),
        ],
        out_specs=row_spec if final else [row_spec, row_spec],
        scratch_shapes=[pltpu.VMEM((t, D_MODEL // 2), jnp.uint32), pltpu.VMEM((t, D_MODEL // 2), jnp.uint32),
                        pltpu.SemaphoreType.DMA(())],
    )
    if final:
        out_shape = jax.ShapeDtypeStruct((n, D_MODEL), F32)
    else:
        out_shape = [jax.ShapeDtypeStruct((n, D_MODEL), F32), jax.ShapeDtypeStruct((n, D_MODEL), BF16)]
    return pl.pallas_call(
        functools.partial(_combine_kernel, final=final),
        grid_spec=grid_spec,
        out_shape=out_shape,
        compiler_params=_cparams(("arbitrary",)),
        name="moe_combine_final" if final else "moe_combine",
    )(pos0, pos1, x, mod5, w0, w1, norm_g, mod5, mod5, ys)


def _moe_layer(x, norm_g4, mod5, router_wt, router_b, w_gate, w_up, w_down, layer, out_norm_g, next_layer):
    hp, logits_t = _norm2(x, norm_g4, mod5, router_wt, layer)
    idx, wts, counts = _router(logits_t, router_b)
    base, item_src, expert, nblk, rows = _plan_items(counts[:, 0])
    pos0 = base[idx[0]] + idx[2]
    pos1 = base[idx[1]] + idx[3]
    hs = _dispatch(hp, pos0, pos1)
    a = _gate_up(hs, w_gate, w_up, layer, item_src, expert, nblk, rows)
    ys = _down(a, w_down, layer, item_src, expert, nblk)
    return _combine(x, ys, pos0, pos1, wts[0][:, None], wts[1][:, None], mod5, layer, out_norm_g, next_layer)


def _attn_mixer(h1, x, mod5, layer, j, attn_w_qkv, attn_sink, attn_w_o, rope_cos, rope_sin, need_ctx):
    q_cols = A_HQ * A_DH
    epi = functools.partial(_epi_rope, rope_cols=(0, q_cols + A_HKV * A_DH), scale_cols=q_cols, scale=A_SCALE,
                            n_cols=q_cols + 2 * A_HKV * A_DH)
    extras = [(rope_cos, _rope_spec(TM_PROJ)), (rope_sin, _rope_spec(TM_PROJ))]
    qkv = _proj(h1, 0, attn_w_qkv, j, q_cols + 2 * A_HKV * A_DH, 512, epi, extras, BF16, "attn_qkv_proj")
    o = _gqa_attention(qkv, attn_sink[j].astype(F32), need_ctx)
    return _out_proj_resid(o, attn_w_o, j, x, mod5, layer, "attn_out_proj")


def _fnet_mixer(h1, x, mod5, layer, j, fnet_w_o, need_ctx):
    cc, sc = _dft_mats(F_GROUP_DIM, F_GROUP_DIM ** -0.5)
    w1 = jnp.concatenate([cc, sc], axis=1).astype(BF16)
    outs = []
    for seq_len, row_blk0, active in ((SEQ, 0, True), (CTX_LEN, N_LAT // min(TM, CTX_LEN), need_ctx)):
        if not active:
            continue
        cl, sl = _dft_mats(seq_len, seq_len ** -0.5)
        a2 = jnp.concatenate([cl, -sl], axis=1).astype(BF16)
        z = _fnet_stage1(h1, w1, row_blk0, seq_len, BATCH)
        y = _fnet_stage2(a2, z, seq_len, BATCH)
        outs.append(y.reshape(BATCH * seq_len, D_MODEL))
    y_all = jnp.concatenate(outs, axis=0) if len(outs) > 1 else outs[0]
    return _out_proj_resid(y_all, fnet_w_o, j, x, mod5, layer, "fnet_out_proj")


def _mla_mixer(h1, x, mod5, layer, j, mla_w_in, mla_g_q, mla_g_kv, mla_w_uq, mla_w_ukv, mla_w_o,
               rope_cos, rope_sin, need_ctx):
    g_cat = jnp.concatenate([mla_g_q[j], mla_g_kv[j]]).reshape(1, M_QR + M_KVR)
    extras = [(g_cat, pl.BlockSpec((1, M_QR), lambda jj, i, k: (0, jj)))]
    cqkv = _proj(h1, 0, mla_w_in, j, M_QR + M_KVR, M_QR, _epi_rmsnorm, extras, BF16, "mla_in_proj")
    w_kr = jnp.pad(mla_w_in[j][:, M_QR + M_KVR:], ((0, 0), (0, LANES - M_ROPE)))
    epi_kr = functools.partial(_epi_rope, rope_cols=(0, LANES), scale_cols=0, scale=1.0, n_cols=LANES)
    rope_extras = [(rope_cos, _rope_spec(TM_PROJ)), (rope_sin, _rope_spec(TM_PROJ))]
    kr = _proj(h1, 0, w_kr, 0, LANES, LANES, epi_kr, rope_extras, BF16, "mla_kr_proj")
    w_uq = mla_w_uq[j].reshape(M_QR, M_H, M_NOPE + M_ROPE)
    w_qn = w_uq[:, :, :M_NOPE].reshape(M_QR, M_H * M_NOPE)
    w_qr = jnp.pad(w_uq[:, :, M_NOPE:], ((0, 0), (0, 0), (0, LANES - M_ROPE))).reshape(M_QR, M_H * LANES)
    w_q = jnp.concatenate([w_qn, w_qr], axis=1)
    nope_cols = M_H * M_NOPE
    epi_q = functools.partial(_epi_rope, rope_cols=(nope_cols, nope_cols + M_H * LANES),
                              scale_cols=nope_cols + M_H * LANES, scale=M_SCALE, n_cols=nope_cols + M_H * LANES)
    qcat = _proj(cqkv, 0, w_q, 0, nope_cols + M_H * LANES, 1024, epi_q, rope_extras, BF16, "mla_q_proj")
    kv = _proj(cqkv, 1, mla_w_ukv, j, M_H * (M_NOPE + M_V), 1024, _epi_plain, [], BF16, "mla_kv_proj")
    o = _mla_attention(qcat, kv, kr, latent=True)
    if need_ctx:
        o = jnp.concatenate([o, _mla_attention(qcat, kv, kr, latent=False)], axis=0)
    return _out_proj_resid(o, mla_w_o, j, x, mod5, layer, "mla_out_proj")


def kernel(x, c, ctx, c_ctx, ada_w, ada_b, norm_g, final_g, attn_w_qkv, attn_sink, attn_w_o,
           fnet_w_o, mla_w_in, mla_g_q, mla_g_kv, mla_w_uq, mla_w_ukv, mla_w_o,
           router_w, router_b, moe_w_gate, moe_w_up, moe_w_down):
    xt = jnp.concatenate([x.reshape(N_LAT, D_MODEL), ctx.reshape(N_CTX, D_MODEL)], axis=0)
    src8 = jnp.concatenate([c, c_ctx[None], jnp.zeros((8 - BATCH - 1, D_MODEL), F32)], axis=0)
    mod5 = _ada_all(src8, ada_w, ada_b).reshape(DEPTH, 8, 6, 1, D_MODEL)
    norm_g4 = norm_g.reshape(DEPTH, 2, 1, D_MODEL)
    router_wt = router_w.T
    rope_cos, rope_sin = _rope_tables(TM_PROJ)
    final_g4 = final_g.reshape(1, 1, 1, D_MODEL)
    h1 = _norm1(xt, norm_g4, mod5, 0)
    for i in range(DEPTH):
        need_ctx = i < DEPTH - 1
        kind, j = i % N_MIXERS, i // N_MIXERS
        if kind == 0:
            x_mix = _attn_mixer(h1, xt, mod5, i, j, attn_w_qkv, attn_sink, attn_w_o, rope_cos, rope_sin, need_ctx)
        elif kind == 1:
            x_mix = _fnet_mixer(h1, xt, mod5, i, j, fnet_w_o, need_ctx)
        else:
            x_mix = _mla_mixer(h1, xt, mod5, i, j, mla_w_in, mla_g_q, mla_g_kv, mla_w_uq, mla_w_ukv, mla_w_o,
                               rope_cos, rope_sin, need_ctx)
        moe_args = (x_mix, norm_g4, mod5, router_wt, router_b, moe_w_gate, moe_w_up, moe_w_down, i)
        if need_ctx:
            xt, h1 = _moe_layer(*moe_args, norm_g4, i + 1)
        else:
            out = _moe_layer(*moe_args, final_g4, None)
    return out.reshape(BATCH, SEQ, D_MODEL)
```
